```python
import math
import jax, jax.numpy as jnp
from jax import lax
import numpy as np

D_MODEL = 1024
BATCH = 32
SEQ = 2048
DEPTH = 1

HEAD_DIM = 64
MOBA_HEADS = 8
SB_HEADS = 8
MOBA_WIDTH = MOBA_HEADS * HEAD_DIM
SB_WIDTH = SB_HEADS * HEAD_DIM
N_BRANCHES = 2
GATE_WIDTH = N_BRANCHES * D_MODEL
IN_WIDTH = 3 * MOBA_WIDTH + 3 * SB_WIDTH + GATE_WIDTH
MOBA_BLOCK = 256
MOBA_TOPK = 3
Q_BLOCK = 128
D_FF = 4 * D_MODEL
ROPE_THETA = 10000.0
RMS_EPS = 1e-6
NEG = -1e30

kernel_name = "hybrid_moba_stickbreaking_gated_block"


def _rmsnorm(x, g):
    x32 = x.astype(jnp.float32)
    ms = jnp.mean(x32 * x32, axis=-1, keepdims=True)
    return (x32 * lax.rsqrt(ms + RMS_EPS) * g.astype(jnp.float32)).astype(x.dtype)


def _rope_tables(seq):
    half = HEAD_DIM // 2
    inv_freq = ROPE_THETA ** (-jnp.arange(half, dtype=jnp.float32) * 2.0 / HEAD_DIM)
    ang = jnp.arange(seq, dtype=jnp.float32)[:, None] * inv_freq[None, :]
    ang = jnp.concatenate([ang, ang], axis=-1)
    return jnp.cos(ang), jnp.sin(ang)


def _rope(t, cos, sin):
    t32 = t.astype(jnp.float32)
    half = HEAD_DIM // 2
    rot = jnp.concatenate([-t32[..., half:], t32[..., :half]], axis=-1)
    return (t32 * cos + rot * sin).astype(t.dtype)


def _to_heads(t, n_heads):
    b, s, _ = t.shape
    return t.reshape(b, s, n_heads, HEAD_DIM).transpose(0, 2, 1, 3)


def _from_heads(t):
    b, h, s, d = t.shape
    return t.transpose(0, 2, 1, 3).reshape(b, s, h * d)


def _moba_attention(q, k, v):
    b, h, s, d = q.shape
    nb = -(-s // MOBA_BLOCK)
    pad = nb * MOBA_BLOCK - s
    kp = jnp.pad(k, ((0, 0), (0, 0), (0, pad), (0, 0)))
    vp = jnp.pad(v, ((0, 0), (0, 0), (0, pad), (0, 0)))
    kblk = kp.reshape(b, h, nb, MOBA_BLOCK, d)
    vblk = vp.reshape(b, h, nb, MOBA_BLOCK, d)
    kmean = jnp.mean(kblk.astype(jnp.float32), axis=3)
    topk = min(MOBA_TOPK, nb)
    n_chunks = s // Q_BLOCK
    scale = 1.0 / math.sqrt(d)
    h_idx = jnp.arange(h)[:, None, None]

    def one_seq(args):
        qs, kbs, vbs, kms = args

        def one_chunk(c):
            start = c * Q_BLOCK
            qc = lax.dynamic_slice_in_dim(qs, start, Q_BLOCK, axis=1).astype(jnp.float32)
            qpos = start + jnp.arange(Q_BLOCK)
            own = start // MOBA_BLOCK
            gate = jnp.einsum('hqd,hnd->hqn', qc, kms)
            past = jnp.arange(nb) < own
            gate = jnp.where(past[None, None, :], gate, NEG)
            _, gidx = lax.top_k(gate, topk)
            valid = gidx < own
            ksel = kbs[h_idx, gidx]
            vsel = vbs[h_idx, gidx]
            s_sel = jnp.einsum('hqd,hqtkd->hqtk', qc, ksel.astype(jnp.float32)) * scale
            s_sel = jnp.where(valid[..., None], s_sel, NEG).reshape(h, Q_BLOCK, topk * MOBA_BLOCK)
            kown = lax.dynamic_index_in_dim(kbs, own, axis=1, keepdims=False).astype(jnp.float32)
            vown = lax.dynamic_index_in_dim(vbs, own, axis=1, keepdims=False).astype(jnp.float32)
            s_own = jnp.einsum('hqd,hkd->hqk', qc, kown) * scale
            kpos = own * MOBA_BLOCK + jnp.arange(MOBA_BLOCK)
            s_own = jnp.where(kpos[None, None, :] <= qpos[None, :, None], s_own, NEG)
            p = jax.nn.softmax(jnp.concatenate([s_sel, s_own], axis=-1), axis=-1)
            p_sel = p[..., :topk * MOBA_BLOCK].reshape(h, Q_BLOCK, topk, MOBA_BLOCK)
            p_own = p[..., topk * MOBA_BLOCK:]
            o = (jnp.einsum('hqtk,hqtkd->hqd', p_sel, vsel.astype(jnp.float32))
                 + jnp.einsum('hqk,hkd->hqd', p_own, vown))
            return o.astype(q.dtype)

        out = lax.map(one_chunk, jnp.arange(n_chunks))
        return out.transpose(1, 0, 2, 3).reshape(h, s, d)

    return lax.map(one_seq, (q, kblk, vblk, kmean))


def _stick_breaking_attention(q, k, v):
    b, h, s, d = q.shape
    scale = 1.0 / math.sqrt(d)
    outs = []
    for c in range(s // Q_BLOCK):
        start, end = c * Q_BLOCK, (c + 1) * Q_BLOCK
        qc = q[:, :, start:end].astype(jnp.float32)
        kc = k[:, :, :end].astype(jnp.float32)
        vc = v[:, :, :end].astype(jnp.float32)
        z = jnp.einsum('bhqd,bhkd->bhqk', qc, kc) * scale
        qpos = start + jnp.arange(Q_BLOCK)
        kpos = jnp.arange(end)
        mask = kpos[None, :] < qpos[:, None]
        log_1mb = jnp.where(mask, jax.nn.log_sigmoid(-z), 0.0)
        suffix = lax.cumsum(log_1mb, axis=3, reverse=True) - log_1mb
        a = jnp.where(mask, jnp.exp(jax.nn.log_sigmoid(z) + suffix), 0.0)
        outs.append(jnp.einsum('bhqk,bhkd->bhqd', a, vc).astype(q.dtype))
    return jnp.concatenate(outs, axis=2)


def setup_inputs(seed: int = 0) -> dict:
    key = jax.random.key(seed)
    ks = jax.random.split(key, 12)
    f32 = jnp.float32
    def nrm(k, shape, fan_in):
        return jax.random.normal(k, shape, f32) * (fan_in ** -0.5)
    def gain(k):
        return jnp.ones((DEPTH, D_MODEL), f32) + 0.05 * jax.random.normal(k, (DEPTH, D_MODEL), f32)
    return {
        "x": jax.random.normal(ks[0], (BATCH, SEQ, D_MODEL), f32),
        "g_pre_mix": gain(ks[1]),
        "w_in": nrm(ks[2], (DEPTH, D_MODEL, IN_WIDTH), D_MODEL),
        "b_gate": 0.02 * jax.random.normal(ks[3], (DEPTH, GATE_WIDTH), f32),
        "w_up_moba": nrm(ks[4], (DEPTH, MOBA_WIDTH, D_MODEL), MOBA_WIDTH),
        "w_up_sb": nrm(ks[5], (DEPTH, SB_WIDTH, D_MODEL), SB_WIDTH),
        "w_out": nrm(ks[6], (DEPTH, D_MODEL, D_MODEL), D_MODEL),
        "g_post_mix": gain(ks[7]),
        "g_pre_mlp": gain(ks[8]),
        "w_mlp_in": nrm(ks[9], (DEPTH, D_MODEL, D_FF), D_MODEL),
        "w_mlp_out": nrm(ks[10], (DEPTH, D_FF, D_MODEL), D_FF),
        "g_post_mlp": gain(ks[11]),
    }


def reference(x, g_pre_mix, w_in, b_gate, w_up_moba, w_up_sb, w_out, g_post_mix,
              g_pre_mlp, w_mlp_in, w_mlp_out, g_post_mlp):
    b, s, _ = x.shape
    cos, sin = _rope_tables(s)
    cos, sin = cos.astype(x.dtype), sin.astype(x.dtype)
    split_at = list(np.cumsum([MOBA_WIDTH] * 3 + [SB_WIDTH] * 3))
    for l in range(DEPTH):
        hn = _rmsnorm(x, g_pre_mix[l])
        proj = jnp.einsum('bsd,de->bse', hn, w_in[l])
        qa, ka, va, qb, kb, vb, gates = jnp.split(proj, split_at, axis=-1)
        qa = _rope(_to_heads(qa, MOBA_HEADS), cos, sin)
        ka = _rope(_to_heads(ka, MOBA_HEADS), cos, sin)
        va = _to_heads(va, MOBA_HEADS)
        qb, kb, vb = (_to_heads(t, SB_HEADS) for t in (qb, kb, vb))
        y_moba = _from_heads(_moba_attention(qa, ka, va))
        y_sb = _from_heads(_stick_breaking_attention(qb, kb, vb))
        y_moba = jnp.einsum('bse,ed->bsd', y_moba, w_up_moba[l])
        y_sb = jnp.einsum('bse,ed->bsd', y_sb, w_up_sb[l])
        g = jax.nn.sigmoid((gates + b_gate[l]).astype(jnp.float32)).astype(x.dtype)
        g_moba, g_sb = g[..., :D_MODEL], g[..., D_MODEL:]
        mixed = g_moba * y_moba + g_sb * y_sb
        mix_out = jnp.einsum('bsd,de->bse', mixed, w_out[l])
        x = x + _rmsnorm(mix_out, g_post_mix[l])
        hn = _rmsnorm(x, g_pre_mlp[l])
        u = jnp.einsum('bsd,df->bsf', hn, w_mlp_in[l])
        u = jnp.square(jax.nn.relu(u))
        m = jnp.einsum('bsf,fd->bsd', u, w_mlp_out[l])
        x = x + _rmsnorm(m, g_post_mlp[l])
    return x
```

```python
import functools
import math

import jax
import jax.numpy as jnp
from jax import lax
from jax.experimental import pallas as pl
from jax.experimental.pallas import tpu as pltpu

F32 = jnp.float32
BF16 = jnp.bfloat16

HEAD_DIM = 64
N_HEADS = 8
BRANCH_WIDTH = N_HEADS * HEAD_DIM
KEY_BLOCK = 256
TOPK = 3
ROPE_THETA = 10000.0
RMS_EPS = 1e-6
NEG = -1e30
LANES = 128
HEADS_PER_TILE = LANES // HEAD_DIM
VMEM_LIMIT = 56 * 1024 * 1024

IN_TILE = 512
OUT_TILE = 512


def _dot(a, b):
    return jnp.dot(a, b, preferred_element_type=F32)


def _dot_nt(a, b):
    return lax.dot_general(a, b, (((1,), (1,)), ((), ())), preferred_element_type=F32)


def _rms(x, g):
    ms = jnp.mean(x * x, axis=-1, keepdims=True)
    return x * lax.rsqrt(ms + RMS_EPS) * g


def _inproj_kernel(x_ref, g_ref, w_ref, b_ref, cos_ref, sina_ref, sinb_ref,
                   qa_ref, ka_ref, va_ref, qb_ref, kb_ref, vb_ref, gate_ref, km_ref):
    w = BRANCH_WIDTH
    hn = _rms(x_ref[0], g_ref[...]).astype(BF16)
    reps = w // LANES
    cos = jnp.concatenate([cos_ref[...]] * reps, axis=1)
    sina = jnp.concatenate([sina_ref[...]] * reps, axis=1)
    sinb = jnp.concatenate([sinb_ref[...]] * reps, axis=1)
    scale = 1.0 / math.sqrt(HEAD_DIM)

    def proj(i):
        return _dot(hn, w_ref[:, i * w:(i + 1) * w])

    def rope(t):
        half = HEAD_DIM // 2
        return t * cos + pltpu.roll(t, w - half, 1) * sina + pltpu.roll(t, half, 1) * sinb

    qa_ref[0] = (rope(proj(0)) * scale).astype(BF16)
    ka = rope(proj(1))
    ka_ref[0] = ka.astype(BF16)
    for i in range(ka.shape[0] // KEY_BLOCK):
        blk = ka[i * KEY_BLOCK:(i + 1) * KEY_BLOCK]
        km_ref[0, i] = jnp.sum(blk, axis=0, keepdims=True) * (1.0 / KEY_BLOCK)
    va_ref[0] = proj(2).astype(BF16)
    qb_ref[0] = (proj(3) * scale).astype(BF16)
    kb_ref[0] = proj(4).astype(BF16)
    vb_ref[0] = proj(5).astype(BF16)
    gates = _dot(hn, w_ref[:, 6 * w:]) + b_ref[...]
    gate_ref[0] = jax.nn.sigmoid(gates).astype(BF16)


def _in_projection(x, g_pre, w_in, b_gate, cos, sina, sinb):
    b, s, d = x.shape
    tm = IN_TILE
    w = BRANCH_WIDTH
    gw = w_in.shape[1] - 6 * w
    nkb = s // KEY_BLOCK
    const = lambda i, j: (0, 0)
    tok = lambda i, j: (j, i, 0)
    tab = lambda i, j: (i, 0)
    branch = jax.ShapeDtypeStruct((b, s, w), BF16)
    return pl.pallas_call(
        _inproj_kernel,
        grid=(s // tm, b),
        in_specs=[
            pl.BlockSpec((1, tm, d), tok),
            pl.BlockSpec((1, d), const),
            pl.BlockSpec(w_in.shape, const, pipeline_mode=pl.Buffered(1)),
            pl.BlockSpec((1, gw), const),
            pl.BlockSpec((tm, LANES), tab),
            pl.BlockSpec((tm, LANES), tab),
            pl.BlockSpec((tm, LANES), tab),
        ],
        out_specs=[pl.BlockSpec((1, tm, w), tok)] * 6 + [
            pl.BlockSpec((1, tm, gw), tok),
            pl.BlockSpec((1, tm // KEY_BLOCK, 1, w), lambda i, j: (j, i, 0, 0)),
        ],
        out_shape=[branch] * 6 + [
            jax.ShapeDtypeStruct((b, s, gw), BF16),
            jax.ShapeDtypeStruct((b, nkb, 1, w), F32),
        ],
        compiler_params=pltpu.CompilerParams(
            dimension_semantics=("arbitrary", "arbitrary"), vmem_limit_bytes=VMEM_LIMIT),
    )(x, g_pre, w_in, b_gate, cos, sina, sinb)


def _moba_kernel(q_ref, k_ref, v_ref, km_ref, o_ref):
    qi = pl.program_id(2)
    t = KEY_BLOCK
    q2 = q_ref[0]
    lane = lax.broadcasted_iota(jnp.int32, (t, LANES), 1)
    row = lax.broadcasted_iota(jnp.int32, (t, t), 0)
    col = lax.broadcasted_iota(jnp.int32, (t, t), 1)
    km = km_ref[0]
    nkb = km.shape[0]
    km_hi = km.astype(BF16)
    km_lo = (km - km_hi.astype(F32)).astype(BF16)
    own_start = pl.multiple_of(qi * t, t)
    k_own = k_ref[0, pl.ds(own_start, t), :]
    v_own = v_ref[0, pl.ds(own_start, t), :]

    outs = []
    for h in range(HEADS_PER_TILE):
        in_head = (lane >= h * HEAD_DIM) & (lane < (h + 1) * HEAD_DIM)
        qh = jnp.where(in_head, q2, jnp.zeros_like(q2))
        off = HEAD_DIM * (1 - h)

        def at_rows(a):
            parts = [jnp.zeros((off, LANES), BF16)] if off else []
            parts += [a, jnp.zeros((LANES - off - nkb, LANES), BF16)]
            return jnp.concatenate(parts, axis=0)

        gate = _dot_nt(qh, at_rows(km_hi)) + _dot_nt(qh, at_rows(km_lo))
        blk = lane - off
        is_gate_lane = (blk >= 0) & (blk < nkb)
        past = is_gate_lane & (blk < qi)
        beaten = jnp.zeros((t, LANES), F32)
        for m in range(nkb):
            gm = gate[:, off + m:off + m + 1]
            wins = (gm > gate) | ((gm == gate) & (m < blk))
            beaten = beaten + jnp.where(wins, jnp.where(m < qi, 1.0, 0.0), 0.0)
        sel = past & (beaten < TOPK)
        bias = jnp.where(is_gate_lane & jnp.logical_not(sel), NEG, 0.0)
        q_aug = (qh.astype(F32) + bias).astype(BF16)

        s = _dot_nt(qh, k_own)
        s = jnp.where(col <= row, s, NEG)
        m_i = jnp.max(s, axis=-1, keepdims=True)
        p = jnp.exp(s - m_i)
        l_i = jnp.sum(p, axis=-1, keepdims=True)
        acc = _dot(p.astype(BF16), v_own)

        def body(n, carry):
            m_i, l_i, acc = carry
            start = pl.multiple_of(n * t, t)
            k2 = k_ref[0, pl.ds(start, t), :]
            v2 = v_ref[0, pl.ds(start, t), :]
            onehot = jnp.where(blk == n, 1.0, 0.0).astype(BF16)
            k_aug = jnp.where(in_head, k2, onehot)
            s = _dot_nt(q_aug, k_aug)
            m_new = jnp.maximum(m_i, jnp.max(s, axis=-1, keepdims=True))
            alpha = jnp.exp(m_i - m_new)
            p = jnp.exp(s - m_new)
            l_new = alpha * l_i + jnp.sum(p, axis=-1, keepdims=True)
            acc_new = alpha * acc + _dot(p.astype(BF16), v2)
            return m_new, l_new, acc_new

        m_i, l_i, acc = lax.fori_loop(0, qi, body, (m_i, l_i, acc))
        outs.append((acc / l_i, in_head))

    y = outs[0][0]
    for o, in_head in outs[1:]:
        y = jnp.where(in_head, o, y)
    o_ref[0] = y.astype(o_ref.dtype)


def _attention_call(kernel, q, k, v, extra_inputs, extra_specs):
    b, s, w = q.shape
    t = KEY_BLOCK
    qmap = lambda bi, hp, qi: (bi, qi, hp)
    kvmap = lambda bi, hp, qi: (bi, 0, hp)
    return pl.pallas_call(
        kernel,
        grid=(b, w // LANES, s // t),
        in_specs=[
            pl.BlockSpec((1, t, LANES), qmap),
            pl.BlockSpec((1, s, LANES), kvmap),
            pl.BlockSpec((1, s, LANES), kvmap),
        ] + extra_specs,
        out_specs=pl.BlockSpec((1, t, LANES), qmap),
        out_shape=jax.ShapeDtypeStruct((b, s, w), BF16),
        compiler_params=pltpu.CompilerParams(
            dimension_semantics=("arbitrary", "arbitrary", "arbitrary"),
            vmem_limit_bytes=VMEM_LIMIT),
    )(q, k, v, *extra_inputs)


def _sb_kernel(q_ref, k_ref, v_ref, o_ref):
    qi = pl.program_id(2)
    t = KEY_BLOCK
    q2 = q_ref[0]
    lane = lax.broadcasted_iota(jnp.int32, (t, LANES), 1)
    row = lax.broadcasted_iota(jnp.int32, (t, t), 0)
    col = lax.broadcasted_iota(jnp.int32, (t, t), 1)
    strict = col < row
    tri = jnp.where(row > col, 1.0, 0.0).astype(BF16)

    def block(qh, start, diag):
        k2 = k_ref[0, pl.ds(start, t), :]
        v2 = v_ref[0, pl.ds(start, t), :]
        z = _dot_nt(qh, k2)
        log_beta = jnp.minimum(z, 0.0) - jnp.log(1.0 + jnp.exp(-jnp.abs(z)))
        log_1mb = log_beta - z
        if diag:
            log_1mb = jnp.where(strict, log_1mb, 0.0)
        hi = log_1mb.astype(BF16)
        lo = (log_1mb - hi.astype(F32)).astype(BF16)
        suffix = _dot(hi, tri) + _dot(lo, tri)
        a = jnp.exp(log_beta + suffix)
        if diag:
            a = jnp.where(strict, a, 0.0)
        total = suffix[:, 0:1] + log_1mb[:, 0:1]
        return _dot(a.astype(BF16), v2), total

    outs = []
    for h in range(HEADS_PER_TILE):
        in_head = (lane >= h * HEAD_DIM) & (lane < (h + 1) * HEAD_DIM)
        qh = jnp.where(in_head, q2, jnp.zeros_like(q2))
        acc, carry = block(qh, pl.multiple_of(qi * t, t), True)

        def body(i, c, qh=qh):
            acc, carry = c
            n = qi - 1 - i
            pv, total = block(qh, pl.multiple_of(n * t, t), False)
            return acc + jnp.exp(carry) * pv, carry + total

        acc, carry = lax.fori_loop(0, qi, body, (acc, carry))
        outs.append((acc, in_head))

    y = outs[0][0]
    for o, in_head in outs[1:]:
        y = jnp.where(in_head, o, y)
    o_ref[0] = y.astype(o_ref.dtype)


def _out_kernel(x_ref, ym_ref, ys_ref, gate_ref, wum_ref, wus_ref, wo_ref, gpm_ref,
                gpre_ref, wmi_ref, wmo_ref, gpost_ref, o_ref):
    d = x_ref.shape[-1]
    up_m = _dot(ym_ref[...], wum_ref[...])
    up_s = _dot(ys_ref[...], wus_ref[...])
    mixed = gate_ref[:, :d].astype(F32) * up_m + gate_ref[:, d:].astype(F32) * up_s
    mix_out = _dot(mixed.astype(BF16), wo_ref[...])
    x1 = x_ref[...] + _rms(mix_out, gpm_ref[...])
    hn = _rms(x1, gpre_ref[...]).astype(BF16)
    ff = wmi_ref.shape[1]
    acc = jnp.zeros_like(x1)
    for j in range(ff // d):
        u = _dot(hn, wmi_ref[:, j * d:(j + 1) * d])
        u = jnp.square(jnp.maximum(u, 0.0))
        acc = acc + _dot(u.astype(BF16), wmo_ref[j * d:(j + 1) * d, :])
    o_ref[...] = x1 + _rms(acc, gpost_ref[...])


def _output_block(x2, ym, ys, gate, wum, wus, wo, gpm, gpre, wmi, wmo, gpost):
    n, d = x2.shape
    tm = OUT_TILE
    tok = lambda i: (i, 0)
    const = lambda i: (0, 0)
    resident = lambda a: pl.BlockSpec(a.shape, const, pipeline_mode=pl.Buffered(1))
    return pl.pallas_call(
        _out_kernel,
        grid=(n // tm,),
        in_specs=[
            pl.BlockSpec((tm, d), tok),
            pl.BlockSpec((tm, ym.shape[1]), tok),
            pl.BlockSpec((tm, ys.shape[1]), tok),
            pl.BlockSpec((tm, gate.shape[1]), tok),
            resident(wum), resident(wus), resident(wo), resident(gpm),
            resident(gpre), resident(wmi), resident(wmo), resident(gpost),
        ],
        out_specs=pl.BlockSpec((tm, d), tok),
        out_shape=jax.ShapeDtypeStruct((n, d), F32),
        compiler_params=pltpu.CompilerParams(
            dimension_semantics=("arbitrary",), vmem_limit_bytes=VMEM_LIMIT),
    )(x2, ym, ys, gate, wum, wus, wo, gpm, gpre, wmi, wmo, gpost)


def _rope_tables(seq):
    half = HEAD_DIM // 2
    inv_freq = ROPE_THETA ** (-jnp.arange(half, dtype=F32) * 2.0 / HEAD_DIM)
    ang = jnp.arange(seq, dtype=F32)[:, None] * inv_freq[None, :]
    ang = jnp.concatenate([ang] * (2 * HEADS_PER_TILE), axis=-1)
    cos, sin = jnp.cos(ang), jnp.sin(ang)
    first_half = (jnp.arange(LANES) % HEAD_DIM) < half
    return cos, jnp.where(first_half, -sin, 0.0), jnp.where(first_half, 0.0, sin)


def kernel(x, g_pre_mix, w_in, b_gate, w_up_moba, w_up_sb, w_out, g_post_mix,
           g_pre_mlp, w_mlp_in, w_mlp_out, g_post_mlp):
    b, s, d = x.shape
    depth = w_in.shape[0]
    cos, sina, sinb = _rope_tables(s)
    for l in range(depth):
        qa, ka, va, qb, kb, vb, gate, kmean = _in_projection(
            x, g_pre_mix[l][None], w_in[l].astype(BF16), b_gate[l][None], cos, sina, sinb)
        nkb = kmean.shape[1]
        y_moba = _attention_call(
            _moba_kernel, qa, ka, va, [kmean.reshape(b, nkb, -1)],
            [pl.BlockSpec((1, nkb, LANES), lambda bi, hp, qi: (bi, 0, hp))])
        y_sb = _attention_call(_sb_kernel, qb, kb, vb, [], [])
        x = _output_block(
            x.reshape(b * s, d), y_moba.reshape(b * s, -1), y_sb.reshape(b * s, -1),
            gate.reshape(b * s, -1),
            w_up_moba[l].astype(BF16), w_up_sb[l].astype(BF16), w_out[l].astype(BF16),
            g_post_mix[l][None], g_pre_mlp[l][None],
            w_mlp_in[l].astype(BF16), w_mlp_out[l].astype(BF16), g_post_mlp[l][None],
        ).reshape(b, s, d)
    return x
```

```python
import math

import jax
import jax.numpy as jnp
from jax import lax
from jax.experimental import pallas as pl
from jax.experimental.pallas import tpu as pltpu

F32 = jnp.float32
BF16 = jnp.bfloat16

HEAD_DIM = 64
N_HEADS = 8
BRANCH_WIDTH = N_HEADS * HEAD_DIM
KEY_BLOCK = 256
TOPK = 3
ROPE_THETA = 10000.0
RMS_EPS = 1e-6
NEG = -1e30
LANES = 128
HEADS_PER_TILE = LANES // HEAD_DIM
VMEM_LIMIT = 56 * 1024 * 1024

GATE_GROUP = 8
GATE_GROUP_SHIFT = 3
SB_DEAD_LOG2 = -150.0
LOG2_E = 1.4426950408889634

IN_TILE = 512
OUT_TILE = 512


def _dot(a, b):
    return jnp.dot(a, b, preferred_element_type=F32)


def _dot_nt(a, b):
    return lax.dot_general(a, b, (((1,), (1,)), ((), ())), preferred_element_type=F32)


def _rms(x, g):
    ms = jnp.mean(x * x, axis=-1, keepdims=True)
    return x * lax.rsqrt(ms + RMS_EPS) * g


def _inproj_kernel(x_ref, g_ref, w_ref, b_ref, cos_ref, sina_ref, sinb_ref,
                   qa_ref, ka_ref, va_ref, qb_ref, kb_ref, vb_ref, gate_ref, km_ref):
    w = BRANCH_WIDTH
    hn = _rms(x_ref[0], g_ref[...]).astype(BF16)
    reps = w // LANES
    cos = jnp.concatenate([cos_ref[...]] * reps, axis=1)
    sina = jnp.concatenate([sina_ref[...]] * reps, axis=1)
    sinb = jnp.concatenate([sinb_ref[...]] * reps, axis=1)
    scale = LOG2_E / math.sqrt(HEAD_DIM)

    def proj(i):
        return _dot(hn, w_ref[:, i * w:(i + 1) * w])

    def rope(t):
        half = HEAD_DIM // 2
        return t * cos + pltpu.roll(t, w - half, 1) * sina + pltpu.roll(t, half, 1) * sinb

    qa_ref[0] = (rope(proj(0)) * scale).astype(BF16)
    ka = rope(proj(1))
    ka_ref[0] = ka.astype(BF16)
    for i in range(ka.shape[0] // KEY_BLOCK):
        blk = ka[i * KEY_BLOCK:(i + 1) * KEY_BLOCK]
        km_ref[0, i] = jnp.sum(blk, axis=0, keepdims=True) * (1.0 / KEY_BLOCK)
    va_ref[0] = proj(2).astype(BF16)
    qb_ref[0] = (proj(3) * scale).astype(BF16)
    kb_ref[0] = proj(4).astype(BF16)
    vb_ref[0] = proj(5).astype(BF16)
    gates = _dot(hn, w_ref[:, 6 * w:]) + b_ref[...]
    gate_ref[0] = jax.nn.sigmoid(gates).astype(BF16)


def _in_projection(x, g_pre, w_in, b_gate, cos, sina, sinb):
    b, s, d = x.shape
    tm = IN_TILE
    w = BRANCH_WIDTH
    gw = w_in.shape[1] - 6 * w
    nkb = s // KEY_BLOCK
    const = lambda i, j: (0, 0)
    tok = lambda i, j: (j, i, 0)
    tab = lambda i, j: (i, 0)
    branch = jax.ShapeDtypeStruct((b, s, w), BF16)
    return pl.pallas_call(
        _inproj_kernel,
        grid=(s // tm, b),
        in_specs=[
            pl.BlockSpec((1, tm, d), tok),
            pl.BlockSpec((1, d), const),
            pl.BlockSpec(w_in.shape, const, pipeline_mode=pl.Buffered(1)),
            pl.BlockSpec((1, gw), const),
            pl.BlockSpec((tm, LANES), tab),
            pl.BlockSpec((tm, LANES), tab),
            pl.BlockSpec((tm, LANES), tab),
        ],
        out_specs=[pl.BlockSpec((1, tm, w), tok)] * 6 + [
            pl.BlockSpec((1, tm, gw), tok),
            pl.BlockSpec((1, tm // KEY_BLOCK, 1, w), lambda i, j: (j, i, 0, 0)),
        ],
        out_shape=[branch] * 6 + [
            jax.ShapeDtypeStruct((b, s, gw), BF16),
            jax.ShapeDtypeStruct((b, nkb, 1, w), F32),
        ],
        compiler_params=pltpu.CompilerParams(
            dimension_semantics=("arbitrary", "arbitrary"), vmem_limit_bytes=VMEM_LIMIT),
    )(x, g_pre, w_in, b_gate, cos, sina, sinb)


def _head_masks(shape):
    lane = lax.broadcasted_iota(jnp.int32, shape, len(shape) - 1)
    return [(lane >= h * HEAD_DIM) & (lane < (h + 1) * HEAD_DIM) for h in range(HEADS_PER_TILE)]


def _merge_heads(outs, masks):
    y = outs[0]
    for o, m in zip(outs[1:], masks[1:]):
        y = jnp.where(m, o, y)
    return y


def _gate_lane_offset(h):
    return HEAD_DIM * (HEADS_PER_TILE - 1 - h)


def _moba_bias(q_ref, km_ref, t, n_tiles):
    lane = lax.broadcasted_iota(jnp.int32, (t, LANES), 1)
    masks = _head_masks((n_tiles, LANES))
    km = km_ref[0]
    rows_hi, rows_lo = [], []
    for i in range(n_tiles):
        parts_hi, parts_lo = [], []
        for h in reversed(range(HEADS_PER_TILE)):
            kmh = jnp.where(masks[h], km, 0.0)
            hi = kmh.astype(BF16)
            lo = (kmh - hi.astype(F32)).astype(BF16)
            before = GATE_GROUP * i
            after = HEAD_DIM - before - n_tiles
            for parts, val in ((parts_hi, hi), (parts_lo, lo)):
                if before:
                    parts.append(jnp.zeros((before, LANES), BF16))
                parts.append(val)
                if after:
                    parts.append(jnp.zeros((after, LANES), BF16))
        rows_hi.append(jnp.concatenate(parts_hi, axis=0))
        rows_lo.append(jnp.concatenate(parts_lo, axis=0))
    w_hi = jnp.concatenate(rows_hi, axis=1)
    w_lo = jnp.concatenate(rows_lo, axis=1)
    q_all = jnp.concatenate([q_ref[0, i * t:(i + 1) * t, :] for i in range(n_tiles)], axis=1)
    gate = _dot_nt(q_all, w_hi) + _dot_nt(q_all, w_lo)

    blk = lane & (GATE_GROUP - 1)
    tile = (lane >> GATE_GROUP_SHIFT) & (GATE_GROUP - 1)
    beaten = jnp.zeros((t, LANES), F32)
    for d in range(1, GATE_GROUP):
        wraps = blk + d >= GATE_GROUP
        other = jnp.where(wraps, pltpu.roll(gate, GATE_GROUP - d, 1), pltpu.roll(gate, LANES - d, 1))
        other_blk = jnp.where(wraps, blk + d - GATE_GROUP, blk + d)
        wins = (other > gate) | ((other == gate) & wraps)
        beaten = beaten + jnp.where(wins & (other_blk < tile), 1.0, 0.0)
    dropped = (blk < tile) & (beaten >= TOPK)
    return jnp.where(dropped, NEG, 0.0)


def _moba_kernel(q_ref, k_ref, v_ref, km_ref, o_ref, kaug_ref):
    t = KEY_BLOCK
    n_tiles = q_ref.shape[1] // t
    assert n_tiles == GATE_GROUP
    lane = lax.broadcasted_iota(jnp.int32, (t, LANES), 1)
    masks = _head_masks((t, LANES))
    row = lax.broadcasted_iota(jnp.int32, (t, t), 0)
    col = lax.broadcasted_iota(jnp.int32, (t, t), 1)
    causal = col <= row

    bias_all = _moba_bias(q_ref, km_ref, t, n_tiles)

    for h in range(HEADS_PER_TILE):
        off = _gate_lane_offset(h)
        for n in range(n_tiles):
            onehot = jnp.where(lane == off + n, 1.0, 0.0).astype(BF16)
            kaug_ref[h, n * t:(n + 1) * t, :] = jnp.where(masks[h], k_ref[0, n * t:(n + 1) * t, :], onehot)

    for qi in range(n_tiles):
        q2 = q_ref[0, qi * t:(qi + 1) * t, :]
        nk = (qi + 1) * t
        if qi:
            bias_qi = pltpu.roll(bias_all, LANES - GATE_GROUP * qi, 1)
        outs = []
        for h in range(HEADS_PER_TILE):
            off = _gate_lane_offset(h)
            qh = jnp.where(masks[h], q2, jnp.zeros_like(q2))
            if qi:
                in_group = (lane >= off) & (lane < off + GATE_GROUP)
                qh = (qh.astype(F32) + jnp.where(in_group, bias_qi, 0.0)).astype(BF16)
            s = _dot_nt(qh, kaug_ref[h, 0:nk, :])
            s_own = jnp.where(causal, s[:, qi * t:], NEG)
            m = jnp.max(s_own, axis=-1, keepdims=True)
            if qi:
                s_past = s[:, :qi * t]
                m = jnp.maximum(m, jnp.max(s_past, axis=-1, keepdims=True))
                p = jnp.concatenate([jnp.exp2(s_past - m), jnp.exp2(s_own - m)], axis=1)
            else:
                p = jnp.exp2(s_own - m)
            l = jnp.sum(p, axis=-1, keepdims=True)
            outs.append(_dot(p.astype(BF16), v_ref[0, 0:nk, :]) / l)
        o_ref[0, qi * t:(qi + 1) * t, :] = _merge_heads(outs, masks).astype(o_ref.dtype)


def _attention_call(kernel, q, k, v, extra_inputs, extra_specs, scratch_shapes):
    b, s, w = q.shape
    blk = lambda bi, hp: (bi, 0, hp)
    spec = pl.BlockSpec((1, s, LANES), blk)
    return pl.pallas_call(
        kernel,
        grid=(b, w // LANES),
        in_specs=[spec, spec, spec] + extra_specs,
        out_specs=spec,
        out_shape=jax.ShapeDtypeStruct((b, s, w), BF16),
        scratch_shapes=scratch_shapes,
        compiler_params=pltpu.CompilerParams(
            dimension_semantics=("arbitrary", "arbitrary"), vmem_limit_bytes=VMEM_LIMIT),
    )(q, k, v, *extra_inputs)


def _sb_kernel(q_ref, k_ref, v_ref, o_ref, acc_ref, carry_ref):
    t = KEY_BLOCK
    n_tiles = q_ref.shape[1] // t
    masks = _head_masks((t, LANES))
    row = lax.broadcasted_iota(jnp.int32, (t, t), 0)
    col = lax.broadcasted_iota(jnp.int32, (t, t), 1)
    strict = col < row
    tri = jnp.where(row > col, 1.0, 0.0).astype(BF16)

    def head_queries(start):
        q2 = q_ref[0, pl.ds(start, t), :]
        return [jnp.where(m, q2, jnp.zeros_like(q2)) for m in masks]

    def block(qh, start, diag):
        k2 = k_ref[0, pl.ds(start, t), :]
        v2 = v_ref[0, pl.ds(start, t), :]
        z = _dot_nt(qh, k2)
        log_beta = jnp.minimum(z, 0.0) - jnp.log2(1.0 + jnp.exp2(-jnp.abs(z)))
        log_1mb = log_beta - z
        if diag:
            log_1mb = jnp.where(strict, log_1mb, 0.0)
        suffix = _dot(log_1mb.astype(BF16), tri)
        a = jnp.exp2(log_beta + suffix)
        if diag:
            a = jnp.where(strict, a, 0.0)
        total = suffix[:, 0:1] + log_1mb[:, 0:1]
        return _dot(a.astype(BF16), v2), jnp.broadcast_to(total, (t, LANES))

    def tile_front(qi, has_prev):
        q_start = pl.multiple_of(qi * t, t)
        state = []
        for qh in head_queries(q_start):
            acc, carry = block(qh, q_start, True)
            if has_prev:
                pv_prev, c_prev = block(qh, pl.multiple_of(q_start - t, t), False)
                acc = acc + jnp.exp2(carry) * pv_prev
                carry = carry + c_prev
            state.append((acc, carry))
        return q_start, state

    def tile_stash(slot, state):
        for h, (acc, carry) in enumerate(state):
            acc_ref[slot, h] = acc
            carry_ref[slot, h] = carry

    def tile_alive(slot):
        alive = jnp.int32(0)
        for h in range(HEADS_PER_TILE):
            alive = jnp.maximum(alive, (jnp.max(carry_ref[slot, h]) > SB_DEAD_LOG2).astype(jnp.int32))
        return alive

    def tile_finish(slot, qi, q_start):
        def more(loop_state):
            n, alive = loop_state
            return (n >= 0) & (alive > 0)

        def earlier_block(loop_state):
            n, _ = loop_state
            start = pl.multiple_of(n * t, t)
            for h, qh in enumerate(head_queries(q_start)):
                pv, total = block(qh, start, False)
                carry = carry_ref[slot, h]
                acc_ref[slot, h] = acc_ref[slot, h] + jnp.exp2(carry) * pv
                carry_ref[slot, h] = carry + total
            return n - 1, tile_alive(slot)

        if not (isinstance(qi, int) and qi < 2):
            lax.while_loop(more, earlier_block, (qi - 2, tile_alive(slot)))
        o_ref[0, pl.ds(q_start, t), :] = _merge_heads(
            [acc_ref[slot, h] for h in range(HEADS_PER_TILE)], masks).astype(o_ref.dtype)

    def tile_pair(first_tile, first_has_prev):
        tiles = [first_tile, first_tile + 1]
        starts = []
        for slot, qi in enumerate(tiles):
            q_start, state = tile_front(qi, first_has_prev or slot > 0)
            tile_stash(slot, state)
            starts.append(q_start)
        for slot, qi in enumerate(tiles):
            tile_finish(slot, qi, starts[slot])

    tile_pair(0, False)

    def pair_body(j, _):
        tile_pair(2 * j, True)
        return 0

    lax.fori_loop(1, n_tiles // 2, pair_body, 0)


def _out_kernel(x_ref, ym_ref, ys_ref, gate_ref, wum_ref, wus_ref, wo_ref, gpm_ref,
                gpre_ref, wmi_ref, wmo_ref, gpost_ref, o_ref):
    d = x_ref.shape[-1]
    up_m = _dot(ym_ref[...], wum_ref[...])
    up_s = _dot(ys_ref[...], wus_ref[...])
    mixed = gate_ref[:, :d].astype(F32) * up_m + gate_ref[:, d:].astype(F32) * up_s
    mix_out = _dot(mixed.astype(BF16), wo_ref[...])
    x1 = x_ref[...] + _rms(mix_out, gpm_ref[...])
    hn = _rms(x1, gpre_ref[...]).astype(BF16)
    ff = wmi_ref.shape[1]
    acc = jnp.zeros_like(x1)
    for j in range(ff // d):
        u = _dot(hn, wmi_ref[:, j * d:(j + 1) * d])
        u = jnp.square(jnp.maximum(u, 0.0))
        acc = acc + _dot(u.astype(BF16), wmo_ref[j * d:(j + 1) * d, :])
    o_ref[...] = x1 + _rms(acc, gpost_ref[...])


def _output_block(x2, ym, ys, gate, wum, wus, wo, gpm, gpre, wmi, wmo, gpost):
    n, d = x2.shape
    tm = OUT_TILE
    tok = lambda i: (i, 0)
    const = lambda i: (0, 0)
    resident = lambda a: pl.BlockSpec(a.shape, const, pipeline_mode=pl.Buffered(1))
    return pl.pallas_call(
        _out_kernel,
        grid=(n // tm,),
        in_specs=[
            pl.BlockSpec((tm, d), tok),
            pl.BlockSpec((tm, ym.shape[1]), tok),
            pl.BlockSpec((tm, ys.shape[1]), tok),
            pl.BlockSpec((tm, gate.shape[1]), tok),
            resident(wum), resident(wus), resident(wo), resident(gpm),
            resident(gpre), resident(wmi), resident(wmo), resident(gpost),
        ],
        out_specs=pl.BlockSpec((tm, d), tok),
        out_shape=jax.ShapeDtypeStruct((n, d), F32),
        compiler_params=pltpu.CompilerParams(
            dimension_semantics=("arbitrary",), vmem_limit_bytes=VMEM_LIMIT),
    )(x2, ym, ys, gate, wum, wus, wo, gpm, gpre, wmi, wmo, gpost)


def _rope_tables(seq):
    half = HEAD_DIM // 2
    inv_freq = ROPE_THETA ** (-jnp.arange(half, dtype=F32) * 2.0 / HEAD_DIM)
    ang = jnp.arange(seq, dtype=F32)[:, None] * inv_freq[None, :]
    ang = jnp.concatenate([ang] * (2 * HEADS_PER_TILE), axis=-1)
    cos, sin = jnp.cos(ang), jnp.sin(ang)
    first_half = (jnp.arange(LANES) % HEAD_DIM) < half
    return cos, jnp.where(first_half, -sin, 0.0), jnp.where(first_half, 0.0, sin)


def kernel(x, g_pre_mix, w_in, b_gate, w_up_moba, w_up_sb, w_out, g_post_mix,
           g_pre_mlp, w_mlp_in, w_mlp_out, g_post_mlp):
    b, s, d = x.shape
    depth = w_in.shape[0]
    cos, sina, sinb = _rope_tables(s)
    for l in range(depth):
        qa, ka, va, qb, kb, vb, gate, kmean = _in_projection(
            x, g_pre_mix[l][None], w_in[l].astype(BF16), b_gate[l][None], cos, sina, sinb)
        nkb = kmean.shape[1]
        y_moba = _attention_call(
            _moba_kernel, qa, ka, va, [kmean.reshape(b, nkb, -1)],
            [pl.BlockSpec((1, nkb, LANES), lambda bi, hp: (bi, 0, hp))],
            [pltpu.VMEM((HEADS_PER_TILE, s, LANES), BF16)])
        y_sb = _attention_call(
            _sb_kernel, qb, kb, vb, [], [],
            [pltpu.VMEM((2, HEADS_PER_TILE, KEY_BLOCK, LANES), F32)] * 2)
        x = _output_block(
            x.reshape(b * s, d), y_moba.reshape(b * s, -1), y_sb.reshape(b * s, -1),
            gate.reshape(b * s, -1),
            w_up_moba[l].astype(BF16), w_up_sb[l].astype(BF16), w_out[l].astype(BF16),
            g_post_mix[l][None], g_pre_mlp[l][None],
            w_mlp_in[l].astype(BF16), w_mlp_out[l].astype(BF16), g_post_mlp[l][None],
        ).reshape(b, s, d)
    return x
```

```python
import math

import jax
import jax.numpy as jnp
from jax import lax
from jax.experimental import pallas as pl
from jax.experimental.pallas import tpu as pltpu

F32 = jnp.float32
BF16 = jnp.bfloat16

HEAD_DIM = 64
N_HEADS = 8
BRANCH_WIDTH = N_HEADS * HEAD_DIM
KEY_BLOCK = 256
TOPK = 3
ROPE_THETA = 10000.0
RMS_EPS = 1e-6
NEG = -1e30
LANES = 128
HEADS_PER_TILE = LANES // HEAD_DIM
VMEM_LIMIT = 56 * 1024 * 1024

GATE_GROUP = 8
GATE_GROUP_SHIFT = 3
SB_DEAD_LOG2 = -126.0
LOG2_E = 1.4426950408889634
SB_TILE_GROUP = 4

IN_TILE = 512
OUT_TILE = 512


def _dot(a, b):
    return jnp.dot(a, b, preferred_element_type=F32)


def _dot_nt(a, b):
    return lax.dot_general(a, b, (((1,), (1,)), ((), ())), preferred_element_type=F32)


def _rms(x, g):
    ms = jnp.mean(x * x, axis=-1, keepdims=True)
    return x * lax.rsqrt(ms + RMS_EPS) * g


def _inproj_kernel(x_ref, g_ref, w_ref, wkbt_ref, b_ref, cos_ref, sina_ref, sinb_ref,
                   qa_ref, kat_ref, va_ref, qb_ref, kbt_ref, vb_ref, gate_ref, km_ref):
    w = BRANCH_WIDTH
    hn = _rms(x_ref[0], g_ref[...]).astype(BF16)
    reps = w // LANES
    cos = jnp.concatenate([cos_ref[...]] * reps, axis=1)
    sina = jnp.concatenate([sina_ref[...]] * reps, axis=1)
    sinb = jnp.concatenate([sinb_ref[...]] * reps, axis=1)
    scale = LOG2_E / math.sqrt(HEAD_DIM)

    def proj(i):
        return _dot(hn, w_ref[:, i * w:(i + 1) * w])

    def rope(t):
        half = HEAD_DIM // 2
        return t * cos + pltpu.roll(t, w - half, 1) * sina + pltpu.roll(t, half, 1) * sinb

    qa_ref[0] = (rope(proj(0)) * scale).astype(BF16)
    ka = rope(proj(1))
    n_blocks = ka.shape[0] // KEY_BLOCK
    for i in range(n_blocks):
        blk = ka[i * KEY_BLOCK:(i + 1) * KEY_BLOCK]
        km_ref[0, i] = jnp.sum(blk, axis=0, keepdims=True) * (1.0 / KEY_BLOCK)
        kat_ref[0, i] = blk.T.astype(BF16)
    va_ref[0] = proj(2).astype(BF16)
    qb_ref[0] = (proj(3) * scale).astype(BF16)
    kbt = _dot_nt(wkbt_ref[...], hn).astype(BF16)
    for i in range(n_blocks):
        kbt_ref[0, i] = kbt[:, i * KEY_BLOCK:(i + 1) * KEY_BLOCK]
    vb_ref[0] = proj(5).astype(BF16)
    gates = _dot(hn, w_ref[:, 6 * w:]) + b_ref[...]
    gate_ref[0] = jax.nn.sigmoid(gates).astype(BF16)


def _in_projection(x, g_pre, w_in, wkbt, b_gate, cos, sina, sinb):
    b, s, d = x.shape
    tm = IN_TILE
    w = BRANCH_WIDTH
    gw = w_in.shape[1] - 6 * w
    nkb = s // KEY_BLOCK
    const = lambda i, j: (0, 0)
    tok = lambda i, j: (j, i, 0)
    tab = lambda i, j: (i, 0)
    branch = jax.ShapeDtypeStruct((b, s, w), BF16)
    branch_spec = pl.BlockSpec((1, tm, w), tok)
    feat_major = jax.ShapeDtypeStruct((b, nkb, w, KEY_BLOCK), BF16)
    feat_major_spec = pl.BlockSpec((1, tm // KEY_BLOCK, w, KEY_BLOCK), lambda i, j: (j, i, 0, 0))
    resident = lambda a: pl.BlockSpec(a.shape, const, pipeline_mode=pl.Buffered(1))
    return pl.pallas_call(
        _inproj_kernel,
        grid=(s // tm, b),
        in_specs=[
            pl.BlockSpec((1, tm, d), tok),
            pl.BlockSpec((1, d), const),
            resident(w_in), resident(wkbt),
            pl.BlockSpec((1, gw), const),
            pl.BlockSpec((tm, LANES), tab),
            pl.BlockSpec((tm, LANES), tab),
            pl.BlockSpec((tm, LANES), tab),
        ],
        out_specs=[branch_spec, feat_major_spec, branch_spec] * 2 + [
            pl.BlockSpec((1, tm, gw), tok),
            pl.BlockSpec((1, tm // KEY_BLOCK, 1, w), lambda i, j: (j, i, 0, 0)),
        ],
        out_shape=[branch, feat_major, branch] * 2 + [
            jax.ShapeDtypeStruct((b, s, gw), BF16),
            jax.ShapeDtypeStruct((b, nkb, 1, w), F32),
        ],
        compiler_params=pltpu.CompilerParams(
            dimension_semantics=("arbitrary", "arbitrary"), vmem_limit_bytes=VMEM_LIMIT),
    )(x, g_pre, w_in, wkbt, b_gate, cos, sina, sinb)


def _head_masks(shape):
    lane = lax.broadcasted_iota(jnp.int32, shape, len(shape) - 1)
    return [(lane >= h * HEAD_DIM) & (lane < (h + 1) * HEAD_DIM) for h in range(HEADS_PER_TILE)]


def _merge_heads(outs, masks):
    y = outs[0]
    for o, m in zip(outs[1:], masks[1:]):
        y = jnp.where(m, o, y)
    return y


def _gate_lane_offset(h):
    return HEAD_DIM * (HEADS_PER_TILE - 1 - h)


def _moba_bias(q_ref, km_ref, t, n_tiles):
    lane = lax.broadcasted_iota(jnp.int32, (t, LANES), 1)
    masks = _head_masks((n_tiles, LANES))
    km = km_ref[0]
    rows_hi, rows_lo = [], []
    for i in range(n_tiles):
        parts_hi, parts_lo = [], []
        for h in reversed(range(HEADS_PER_TILE)):
            kmh = jnp.where(masks[h], km, 0.0)
            hi = kmh.astype(BF16)
            lo = (kmh - hi.astype(F32)).astype(BF16)
            before = GATE_GROUP * i
            after = HEAD_DIM - before - n_tiles
            for parts, val in ((parts_hi, hi), (parts_lo, lo)):
                if before:
                    parts.append(jnp.zeros((before, LANES), BF16))
                parts.append(val)
                if after:
                    parts.append(jnp.zeros((after, LANES), BF16))
        rows_hi.append(jnp.concatenate(parts_hi, axis=0))
        rows_lo.append(jnp.concatenate(parts_lo, axis=0))
    w_hi = jnp.concatenate(rows_hi, axis=1)
    w_lo = jnp.concatenate(rows_lo, axis=1)
    q_all = jnp.concatenate([q_ref[0, i * t:(i + 1) * t, :] for i in range(n_tiles)], axis=1)
    gate = _dot_nt(q_all, w_hi) + _dot_nt(q_all, w_lo)
    yield None

    blk = lane & (GATE_GROUP - 1)
    tile = (lane >> GATE_GROUP_SHIFT) & (GATE_GROUP - 1)
    beaten = jnp.zeros((t, LANES), F32)
    for d in range(1, GATE_GROUP):
        wraps = blk + d >= GATE_GROUP
        other = jnp.where(wraps, pltpu.roll(gate, GATE_GROUP - d, 1), pltpu.roll(gate, LANES - d, 1))
        other_blk = jnp.where(wraps, blk + d - GATE_GROUP, blk + d)
        wins = (other > gate) | ((other == gate) & wraps)
        beaten = beaten + jnp.where(wins & (other_blk < tile), 1.0, 0.0)
        yield None
    dropped = (blk < tile) & (beaten >= TOPK)
    yield jnp.where(dropped, NEG, 0.0)


def _moba_kernel(q_ref, kt_ref, v_ref, km_ref, o_ref, kaug_ref, vaug_ref):
    t = KEY_BLOCK
    n_tiles = q_ref.shape[1] // t
    assert n_tiles == GATE_GROUP
    lane = lax.broadcasted_iota(jnp.int32, (t, LANES), 1)
    masks = _head_masks((t, LANES))
    row = lax.broadcasted_iota(jnp.int32, (t, t), 0)
    col = lax.broadcasted_iota(jnp.int32, (t, t), 1)
    causal = col <= row

    feat = lax.broadcasted_iota(jnp.int32, (LANES, t), 0)
    for h in range(HEADS_PER_TILE):
        off = _gate_lane_offset(h)
        in_head = (feat >= h * HEAD_DIM) & (feat < (h + 1) * HEAD_DIM)
        for n in range(n_tiles):
            onehot = jnp.where(feat == off + n, 1.0, 0.0).astype(BF16)
            kaug_ref[h, :, n * t:(n + 1) * t] = jnp.where(in_head, kt_ref[0, n], onehot)

    for h in range(HEADS_PER_TILE):
        for n in range(n_tiles):
            vaug_ref[h, n * t:(n + 1) * t, :] = jnp.where(
                masks[h], v_ref[0, n * t:(n + 1) * t, :], jnp.ones((t, LANES), BF16))

    bias_steps = _moba_bias(q_ref, km_ref, t, n_tiles)
    bias_all = []

    def scores(qi, h):
        q2 = q_ref[0, qi * t:(qi + 1) * t, :]
        qh = jnp.where(masks[h], q2, jnp.zeros_like(q2))
        if qi > TOPK:
            bias_all.extend(step for step in bias_steps if step is not None)
            off = _gate_lane_offset(h)
            bias_qi = pltpu.roll(bias_all[0], LANES - GATE_GROUP * qi, 1)
            in_group = (lane >= off) & (lane < off + GATE_GROUP)
            qh = (qh.astype(F32) + jnp.where(in_group, bias_qi, 0.0)).astype(BF16)
        return _dot(qh, kaug_ref[h, :, 0:(qi + 1) * t])

    def attend(qi, h, s):
        nk = (qi + 1) * t
        s_own = jnp.where(causal, s[:, qi * t:], NEG)
        m = jnp.max(s_own, axis=-1, keepdims=True)
        if qi:
            s_past = s[:, :qi * t]
            m = jnp.maximum(m, jnp.max(s_past, axis=-1, keepdims=True))
            p = jnp.concatenate([jnp.exp2((s_past - m).astype(BF16)),
                                 jnp.exp2((s_own - m).astype(BF16))], axis=1)
        else:
            p = jnp.exp2((s_own - m).astype(BF16))
        num_den = _dot(p, vaug_ref[h, 0:nk, :])
        return num_den / pltpu.roll(num_den, HEAD_DIM, 1)

    chains = [(qi, h) for qi in range(n_tiles) for h in range(HEADS_PER_TILE)]
    s_next = scores(*chains[0])
    outs = []
    for i, (qi, h) in enumerate(chains):
        s = s_next
        if i + 1 < len(chains):
            s_next = scores(*chains[i + 1])
        outs.append(attend(qi, h, s))
        if not bias_all:
            step = next(bias_steps)
            if step is not None:
                bias_all.append(step)
        if h == HEADS_PER_TILE - 1:
            o_ref[0, qi * t:(qi + 1) * t, :] = _merge_heads(outs, masks).astype(o_ref.dtype)
            outs = []


def _attention_call(kernel, q, kt, v, extra_inputs, extra_specs, scratch_shapes):
    b, s, w = q.shape
    spec = pl.BlockSpec((1, s, LANES), lambda bi, hp: (bi, 0, hp))
    kt_spec = pl.BlockSpec((1, s // KEY_BLOCK, LANES, KEY_BLOCK), lambda bi, hp: (bi, 0, hp, 0))
    return pl.pallas_call(
        kernel,
        grid=(b, w // LANES),
        in_specs=[spec, kt_spec, spec] + extra_specs,
        out_specs=spec,
        out_shape=jax.ShapeDtypeStruct((b, s, w), BF16),
        scratch_shapes=scratch_shapes,
        compiler_params=pltpu.CompilerParams(
            dimension_semantics=("arbitrary", "arbitrary"), vmem_limit_bytes=VMEM_LIMIT),
    )(q, kt, v, *extra_inputs)


def _sb_kernel(q_ref, kt_ref, v_ref, o_ref, acc_ref, carry_ref):
    t = KEY_BLOCK
    half = t // 2
    rows = HEADS_PER_TILE * half
    n_tiles = q_ref.shape[1] // t
    masks = _head_masks((half, LANES))
    tri = jnp.where(lax.broadcasted_iota(jnp.int32, (t, t), 0) > lax.broadcasted_iota(jnp.int32, (t, t), 1),
                    1.0, 0.0).astype(BF16)

    def earlier_keys(nk, first_query):
        q_pos = (lax.broadcasted_iota(jnp.int32, (rows, nk), 0) & (half - 1)) + first_query
        return lax.broadcasted_iota(jnp.int32, (rows, nk), 1) < q_pos

    def logits(q_start, block, nk):
        q2 = q_ref[0, pl.ds(q_start, half), :]
        qs = jnp.concatenate([jnp.where(m, q2, jnp.zeros_like(q2)) for m in masks], axis=0)
        return _dot(qs, kt_ref[0, block, :, 0:nk])

    def weigh(z, block, nk, earlier):
        v2 = v_ref[0, pl.ds(pl.multiple_of(block * t, t), nk), :]
        log_beta = jnp.minimum(z, 0.0) - jnp.log2(1.0 + jnp.exp2(-jnp.abs(z)))
        log_1mb = log_beta - z
        if earlier is not None:
            log_1mb = jnp.where(earlier, log_1mb, 0.0)
        suffix = _dot(log_1mb.astype(BF16), tri[:nk, :nk])
        a = jnp.exp2(log_beta + suffix)
        if earlier is not None:
            a = jnp.where(earlier, a, 0.0)
        total = suffix[:, 0:1] + log_1mb[:, 0:1]
        return _dot(a.astype(BF16), v2), jnp.broadcast_to(total, (rows, LANES))

    def alive(slot):
        return (jnp.max(carry_ref[slot]) > SB_DEAD_LOG2).astype(jnp.int32)

    def finish(slot, q_start, first_block, live):
        def more(state):
            n, live = state
            return (n >= 0) & (live > 0)

        def earlier_block(state):
            n, _ = state
            pv, total = weigh(logits(q_start, n, t), n, t, None)
            carry = carry_ref[slot]
            acc_ref[slot] = acc_ref[slot] + jnp.exp2(carry) * pv
            carry_ref[slot] = carry + total
            return n - 1, alive(slot)

        if first_block >= 0:
            lax.while_loop(more, earlier_block, (jnp.int32(first_block), live))
        acc = acc_ref[slot]
        o_ref[0, q_start:q_start + half, :] = _merge_heads(
            [acc[h * half:(h + 1) * half] for h in range(HEADS_PER_TILE)], masks).astype(o_ref.dtype)

    def tile_group(tiles):
        slabs = []
        pending = []
        for i, qi in enumerate(tiles):
            top = qi * t
            slabs.append((2 * i, top, qi, half, earlier_keys(half, 0)))
            if qi:
                slabs.append((2 * i, top, qi - 1, t, None))
            slabs.append((2 * i + 1, top + half, qi, t, earlier_keys(t, half)))
            pending += [(2 * i, top, qi - 2), (2 * i + 1, top + half, qi - 1)]
        z_next = logits(*slabs[0][1:4])
        started = set()
        for j, (slot, q_start, block, nk, mask) in enumerate(slabs):
            z = z_next
            if j + 1 < len(slabs):
                z_next = logits(*slabs[j + 1][1:4])
            pv, total = weigh(z, block, nk, mask)
            if slot in started:
                carry = carry_ref[slot]
                acc_ref[slot] = acc_ref[slot] + jnp.exp2(carry) * pv
                carry_ref[slot] = carry + total
            else:
                acc_ref[slot] = pv
                carry_ref[slot] = total
                started.add(slot)
        lives = [alive(slot) if first_block >= 0 else None for slot, _, first_block in pending]
        for (slot, q_start, first_block), live in zip(pending, lives):
            finish(slot, q_start, first_block, live)

    for first in range(0, n_tiles, SB_TILE_GROUP):
        tile_group(range(first, first + SB_TILE_GROUP))


def _out_kernel(x_ref, ym_ref, ys_ref, gate_ref, wum_ref, wus_ref, wo_ref, gpm_ref,
                gpre_ref, wmi_ref, wmo_ref, gpost_ref, o_ref):
    d = x_ref.shape[-1]
    up_m = _dot(ym_ref[...], wum_ref[...])
    up_s = _dot(ys_ref[...], wus_ref[...])
    mixed = gate_ref[:, :d].astype(F32) * up_m + gate_ref[:, d:].astype(F32) * up_s
    mix_out = _dot(mixed.astype(BF16), wo_ref[...])
    x1 = x_ref[...] + _rms(mix_out, gpm_ref[...])
    hn = _rms(x1, gpre_ref[...]).astype(BF16)
    ff = wmi_ref.shape[1]
    acc = jnp.zeros_like(x1)
    for j in range(ff // d):
        u = _dot(hn, wmi_ref[:, j * d:(j + 1) * d])
        u = jnp.square(jnp.maximum(u, 0.0))
        acc = acc + _dot(u.astype(BF16), wmo_ref[j * d:(j + 1) * d, :])
    o_ref[...] = x1 + _rms(acc, gpost_ref[...])


def _output_block(x2, ym, ys, gate, wum, wus, wo, gpm, gpre, wmi, wmo, gpost):
    n, d = x2.shape
    tm = OUT_TILE
    tok = lambda i: (i, 0)
    const = lambda i: (0, 0)
    resident = lambda a: pl.BlockSpec(a.shape, const, pipeline_mode=pl.Buffered(1))
    return pl.pallas_call(
        _out_kernel,
        grid=(n // tm,),
        in_specs=[
            pl.BlockSpec((tm, d), tok),
            pl.BlockSpec((tm, ym.shape[1]), tok),
            pl.BlockSpec((tm, ys.shape[1]), tok),
            pl.BlockSpec((tm, gate.shape[1]), tok),
            resident(wum), resident(wus), resident(wo), resident(gpm),
            resident(gpre), resident(wmi), resident(wmo), resident(gpost),
        ],
        out_specs=pl.BlockSpec((tm, d), tok),
        out_shape=jax.ShapeDtypeStruct((n, d), F32),
        compiler_params=pltpu.CompilerParams(
            dimension_semantics=("arbitrary",), vmem_limit_bytes=VMEM_LIMIT),
    )(x2, ym, ys, gate, wum, wus, wo, gpm, gpre, wmi, wmo, gpost)


def _rope_tables(seq):
    half = HEAD_DIM // 2
    inv_freq = ROPE_THETA ** (-jnp.arange(half, dtype=F32) * 2.0 / HEAD_DIM)
    ang = jnp.arange(seq, dtype=F32)[:, None] * inv_freq[None, :]
    ang = jnp.concatenate([ang] * (2 * HEADS_PER_TILE), axis=-1)
    cos, sin = jnp.cos(ang), jnp.sin(ang)
    first_half = (jnp.arange(LANES) % HEAD_DIM) < half
    return cos, jnp.where(first_half, -sin, 0.0), jnp.where(first_half, 0.0, sin)


def kernel(x, g_pre_mix, w_in, b_gate, w_up_moba, w_up_sb, w_out, g_post_mix,
           g_pre_mlp, w_mlp_in, w_mlp_out, g_post_mlp):
    b, s, d = x.shape
    depth = w_in.shape[0]
    cos, sina, sinb = _rope_tables(s)
    for l in range(depth):
        w = BRANCH_WIDTH
        w16 = w_in[l].astype(BF16)
        qa, kat, va, qb, kbt, vb, gate, kmean = _in_projection(
            x, g_pre_mix[l][None], w16, w16[:, 4 * w:5 * w].T, b_gate[l][None], cos, sina, sinb)
        nkb = kmean.shape[1]
        y_moba = _attention_call(
            _moba_kernel, qa, kat, va, [kmean.reshape(b, nkb, -1)],
            [pl.BlockSpec((1, nkb, LANES), lambda bi, hp: (bi, 0, hp))],
            [pltpu.VMEM((HEADS_PER_TILE, LANES, s), BF16), pltpu.VMEM((HEADS_PER_TILE, s, LANES), BF16)])
        y_sb = _attention_call(
            _sb_kernel, qb, kbt, vb, [], [],
            [pltpu.VMEM((2 * SB_TILE_GROUP, KEY_BLOCK, LANES), F32)] * 2)
        x = _output_block(
            x.reshape(b * s, d), y_moba.reshape(b * s, -1), y_sb.reshape(b * s, -1),
            gate.reshape(b * s, -1),
            w_up_moba[l].astype(BF16), w_up_sb[l].astype(BF16), w_out[l].astype(BF16),
            g_post_mix[l][None], g_pre_mlp[l][None],
            w_mlp_in[l].astype(BF16), w_mlp_out[l].astype(BF16), g_post_mlp[l][None],
        ).reshape(b, s, d)
    return x
```

```python
import math

import jax
import jax.numpy as jnp
from jax import lax
from jax.experimental import pallas as pl
from jax.experimental.pallas import tpu as pltpu

F32 = jnp.float32
BF16 = jnp.bfloat16

HEAD_DIM = 64
N_HEADS = 8
BRANCH_WIDTH = N_HEADS * HEAD_DIM
KEY_BLOCK = 256
TOPK = 3
ROPE_THETA = 10000.0
RMS_EPS = 1e-6
NEG = -1e30
LANES = 128
HEADS_PER_TILE = LANES // HEAD_DIM
VMEM_LIMIT = 56 * 1024 * 1024

GATE_GROUP = 8
GATE_GROUP_SHIFT = 3
SB_DEAD_LOG2 = -126.0
LOG2_E = 1.4426950408889634
SB_TILE_GROUP = 8
MOBA_TILE_ORDER = (3, 2, 1, 0, 7, 6, 5, 4)

IN_TILE = 1024
OUT_TILE = 512
OUT_CHAINS = 2


def _dot(a, b):
    return jnp.dot(a, b, preferred_element_type=F32)


def _dot_nt(a, b):
    return lax.dot_general(a, b, (((1,), (1,)), ((), ())), preferred_element_type=F32)


def _rms(x, g):
    ms = jnp.mean(x * x, axis=-1, keepdims=True)
    return x * lax.rsqrt(ms + RMS_EPS) * g


def _inproj_kernel(x_ref, g_ref, w_ref, wkbt_ref, b_ref, cos_ref, sina_ref, sinb_ref,
                   qa_ref, kat_ref, va_ref, qb_ref, kbt_ref, vb_ref, gate_ref, km_ref):
    w = BRANCH_WIDTH
    reps = w // LANES
    scale = LOG2_E / math.sqrt(HEAD_DIM)
    half = HEAD_DIM // 2
    blocks = [pl.ds(i * KEY_BLOCK, KEY_BLOCK) for i in range(x_ref.shape[1] // KEY_BLOCK)]
    hns = [_rms(x_ref[0, r, :], g_ref[...]).astype(BF16) for r in blocks]

    tables = [[jnp.concatenate([ref[r, :]] * reps, axis=1) for ref in (cos_ref, sina_ref, sinb_ref)]
              for r in blocks]

    def proj(hn, c):
        return _dot(hn, w_ref[:, c * w:(c + 1) * w])

    def rope(t, i):
        cos, sina, sinb = tables[i]
        return t * cos + pltpu.roll(t, w - half, 1) * sina + pltpu.roll(t, half, 1) * sinb

    chains = list(enumerate(zip(blocks, hns)))
    for i, (r, hn) in chains:
        qa_ref[0, r, :] = (rope(proj(hn, 0), i) * scale).astype(BF16)
    for i, (r, hn) in chains:
        ka = rope(proj(hn, 1), i)
        km_ref[0, i] = jnp.sum(ka, axis=0, keepdims=True) * (1.0 / KEY_BLOCK)
        kat_ref[0, i] = ka.T.astype(BF16)
    for i, (r, hn) in chains:
        va_ref[0, r, :] = proj(hn, 2).astype(BF16)
    for i, (r, hn) in chains:
        qb_ref[0, r, :] = (proj(hn, 3) * scale).astype(BF16)
    for i, (r, hn) in chains:
        kbt_ref[0, i] = _dot_nt(wkbt_ref[...], hn).astype(BF16)
    for i, (r, hn) in chains:
        vb_ref[0, r, :] = proj(hn, 5).astype(BF16)
    for i, (r, hn) in chains:
        gates = _dot(hn, w_ref[:, 6 * w:]) + b_ref[...]
        gate_ref[0, r, :] = jax.nn.sigmoid(gates).astype(BF16)


def _in_projection(x, g_pre, w_in, wkbt, b_gate, cos, sina, sinb):
    b, s, d = x.shape
    tm = IN_TILE
    w = BRANCH_WIDTH
    gw = w_in.shape[1] - 6 * w
    nkb = s // KEY_BLOCK
    const = lambda i, j: (0, 0)
    tok = lambda i, j: (j, i, 0)
    tab = lambda i, j: (i, 0)
    branch = jax.ShapeDtypeStruct((b, s, w), BF16)
    branch_spec = pl.BlockSpec((1, tm, w), tok)
    feat_major = jax.ShapeDtypeStruct((b, nkb, w, KEY_BLOCK), BF16)
    feat_major_spec = pl.BlockSpec((1, tm // KEY_BLOCK, w, KEY_BLOCK), lambda i, j: (j, i, 0, 0))
    resident = lambda a: pl.BlockSpec(a.shape, const, pipeline_mode=pl.Buffered(1))
    return pl.pallas_call(
        _inproj_kernel,
        grid=(s // tm, b),
        in_specs=[
            pl.BlockSpec((1, tm, d), tok),
            pl.BlockSpec((1, d), const),
            resident(w_in), resident(wkbt),
            pl.BlockSpec((1, gw), const),
            pl.BlockSpec((tm, LANES), tab),
            pl.BlockSpec((tm, LANES), tab),
            pl.BlockSpec((tm, LANES), tab),
        ],
        out_specs=[branch_spec, feat_major_spec, branch_spec] * 2 + [
            pl.BlockSpec((1, tm, gw), tok),
            pl.BlockSpec((1, tm // KEY_BLOCK, 1, w), lambda i, j: (j, i, 0, 0)),
        ],
        out_shape=[branch, feat_major, branch] * 2 + [
            jax.ShapeDtypeStruct((b, s, gw), BF16),
            jax.ShapeDtypeStruct((b, nkb, 1, w), F32),
        ],
        compiler_params=pltpu.CompilerParams(
            dimension_semantics=("arbitrary", "arbitrary"), vmem_limit_bytes=VMEM_LIMIT),
    )(x, g_pre, w_in, wkbt, b_gate, cos, sina, sinb)


def _head_masks(shape):
    lane = lax.broadcasted_iota(jnp.int32, shape, len(shape) - 1)
    return [(lane >= h * HEAD_DIM) & (lane < (h + 1) * HEAD_DIM) for h in range(HEADS_PER_TILE)]


def _merge_heads(outs, masks):
    y = outs[0]
    for o, m in zip(outs[1:], masks[1:]):
        y = jnp.where(m, o, y)
    return y


def _gate_lane_offset(h):
    return HEAD_DIM * (HEADS_PER_TILE - 1 - h)


def _moba_bias(q_ref, km_ref, t, n_tiles):
    lane = lax.broadcasted_iota(jnp.int32, (t, LANES), 1)
    masks = _head_masks((n_tiles, LANES))
    km = km_ref[0]
    rows_hi, rows_lo = [], []
    for i in range(n_tiles):
        parts_hi, parts_lo = [], []
        for h in reversed(range(HEADS_PER_TILE)):
            kmh = jnp.where(masks[h], km, 0.0)
            hi = kmh.astype(BF16)
            lo = (kmh - hi.astype(F32)).astype(BF16)
            before = GATE_GROUP * i
            after = HEAD_DIM - before - n_tiles
            for parts, val in ((parts_hi, hi), (parts_lo, lo)):
                if before:
                    parts.append(jnp.zeros((before, LANES), BF16))
                parts.append(val)
                if after:
                    parts.append(jnp.zeros((after, LANES), BF16))
        rows_hi.append(jnp.concatenate(parts_hi, axis=0))
        rows_lo.append(jnp.concatenate(parts_lo, axis=0))
    w_hi = jnp.concatenate(rows_hi, axis=1)
    w_lo = jnp.concatenate(rows_lo, axis=1)
    q_all = jnp.concatenate([q_ref[0, i * t:(i + 1) * t, :] for i in range(n_tiles)], axis=1)
    gate = _dot_nt(q_all, w_hi) + _dot_nt(q_all, w_lo)
    yield None

    blk = lane & (GATE_GROUP - 1)
    tile = (lane >> GATE_GROUP_SHIFT) & (GATE_GROUP - 1)
    beaten = jnp.zeros((t, LANES), F32)
    for d in range(1, GATE_GROUP):
        wraps = blk + d >= GATE_GROUP
        other = jnp.where(wraps, pltpu.roll(gate, GATE_GROUP - d, 1), pltpu.roll(gate, LANES - d, 1))
        other_blk = jnp.where(wraps, blk + d - GATE_GROUP, blk + d)
        wins = (other > gate) | ((other == gate) & wraps)
        beaten = beaten + jnp.where(wins & (other_blk < tile), 1.0, 0.0)
        yield None
    dropped = (blk < tile) & (beaten >= TOPK)
    yield jnp.where(dropped, NEG, 0.0)


def _moba_kernel(q_ref, kt_ref, v_ref, km_ref, o_ref, kaug_ref, vaug_ref):
    t = KEY_BLOCK
    n_tiles = q_ref.shape[1] // t
    assert n_tiles == GATE_GROUP
    lane = lax.broadcasted_iota(jnp.int32, (t, LANES), 1)
    masks = _head_masks((t, LANES))
    row = lax.broadcasted_iota(jnp.int32, (t, t), 0)
    col = lax.broadcasted_iota(jnp.int32, (t, t), 1)
    causal = col <= row

    feat = lax.broadcasted_iota(jnp.int32, (LANES, t), 0)
    for h in range(HEADS_PER_TILE):
        off = _gate_lane_offset(h)
        in_head = (feat >= h * HEAD_DIM) & (feat < (h + 1) * HEAD_DIM)
        for n in range(n_tiles):
            onehot = jnp.where(feat == off + n, 1.0, 0.0).astype(BF16)
            kaug_ref[h, :, n * t:(n + 1) * t] = jnp.where(in_head, kt_ref[0, n], onehot)

    for h in range(HEADS_PER_TILE):
        for n in range(n_tiles):
            vaug_ref[h, n * t:(n + 1) * t, :] = jnp.where(
                masks[h], v_ref[0, n * t:(n + 1) * t, :], jnp.ones((t, LANES), BF16))

    bias_steps = _moba_bias(q_ref, km_ref, t, n_tiles)
    bias_all = []

    def scores(qi, h):
        q2 = q_ref[0, qi * t:(qi + 1) * t, :]
        qh = jnp.where(masks[h], q2, jnp.zeros_like(q2))
        if qi > TOPK:
            bias_all.extend(step for step in bias_steps if step is not None)
            off = _gate_lane_offset(h)
            bias_qi = pltpu.roll(bias_all[0], LANES - GATE_GROUP * qi, 1)
            in_group = (lane >= off) & (lane < off + GATE_GROUP)
            qh = (qh.astype(F32) + jnp.where(in_group, bias_qi, 0.0)).astype(BF16)
        return _dot(qh, kaug_ref[h, :, 0:(qi + 1) * t])

    def attend(qi, h, s):
        nk = (qi + 1) * t
        s_own = jnp.where(causal, s[:, qi * t:], NEG)
        m = jnp.max(s_own, axis=-1, keepdims=True)
        if qi:
            s_past = s[:, :qi * t]
            m = jnp.maximum(m, jnp.max(s_past, axis=-1, keepdims=True))
            p = jnp.concatenate([jnp.exp2((s_past - m).astype(BF16)),
                                 jnp.exp2((s_own - m).astype(BF16))], axis=1)
        else:
            p = jnp.exp2((s_own - m).astype(BF16))
        num_den = _dot(p, vaug_ref[h, 0:nk, :])
        return num_den / pltpu.roll(num_den, HEAD_DIM, 1)

    chains = [(qi, h) for qi in MOBA_TILE_ORDER for h in range(HEADS_PER_TILE)]
    s_next = scores(*chains[0])
    outs = []
    for i, (qi, h) in enumerate(chains):
        s = s_next
        if i + 1 < len(chains):
            s_next = scores(*chains[i + 1])
        outs.append(attend(qi, h, s))
        if not bias_all:
            step = next(bias_steps)
            if step is not None:
                bias_all.append(step)
        if h == HEADS_PER_TILE - 1:
            o_ref[0, qi * t:(qi + 1) * t, :] = _merge_heads(outs, masks).astype(o_ref.dtype)
            outs = []


def _attention_call(kernel, q, kt, v, extra_inputs, extra_specs, scratch_shapes):
    b, s, w = q.shape
    spec = pl.BlockSpec((1, s, LANES), lambda bi, hp: (bi, 0, hp))
    kt_spec = pl.BlockSpec((1, s // KEY_BLOCK, LANES, KEY_BLOCK), lambda bi, hp: (bi, 0, hp, 0))
    return pl.pallas_call(
        kernel,
        grid=(b, w // LANES),
        in_specs=[spec, kt_spec, spec] + extra_specs,
        out_specs=spec,
        out_shape=jax.ShapeDtypeStruct((b, s, w), BF16),
        scratch_shapes=scratch_shapes,
        compiler_params=pltpu.CompilerParams(
            dimension_semantics=("arbitrary", "arbitrary"), vmem_limit_bytes=VMEM_LIMIT),
    )(q, kt, v, *extra_inputs)


def _sb_kernel(q_ref, kt_ref, v_ref, o_ref, acc_ref, carry_ref):
    t = KEY_BLOCK
    half = t // 2
    rows = HEADS_PER_TILE * half
    n_tiles = q_ref.shape[1] // t
    masks = _head_masks((half, LANES))
    tri = jnp.where(lax.broadcasted_iota(jnp.int32, (t, t), 0) > lax.broadcasted_iota(jnp.int32, (t, t), 1),
                    1.0, 0.0).astype(BF16)

    def earlier_keys(nk, first_query):
        q_pos = (lax.broadcasted_iota(jnp.int32, (rows, nk), 0) & (half - 1)) + first_query
        return lax.broadcasted_iota(jnp.int32, (rows, nk), 1) < q_pos

    def logits(q_start, block, nk):
        q2 = q_ref[0, pl.ds(q_start, half), :]
        qs = jnp.concatenate([jnp.where(m, q2, jnp.zeros_like(q2)) for m in masks], axis=0)
        return _dot(qs, kt_ref[0, block, :, 0:nk])

    def weigh(z, block, nk, earlier):
        v2 = v_ref[0, pl.ds(pl.multiple_of(block * t, t), nk), :]
        log_beta = jnp.minimum(z, 0.0) - jnp.log2(1.0 + jnp.exp2(-jnp.abs(z)))
        log_1mb = log_beta - z
        if earlier is not None:
            log_1mb = jnp.where(earlier, log_1mb, 0.0)
        suffix = _dot(log_1mb.astype(BF16), tri[:nk, :nk])
        a = jnp.exp2(log_beta + suffix)
        if earlier is not None:
            a = jnp.where(earlier, a, 0.0)
        total = suffix[:, 0:1] + log_1mb[:, 0:1]
        return _dot(a.astype(BF16), v2), jnp.broadcast_to(total, (rows, LANES))

    def alive(slot):
        return (jnp.max(carry_ref[slot]) > SB_DEAD_LOG2).astype(jnp.int32)

    def finish(slot, q_start, first_block, live):
        def more(state):
            n, live = state
            return (n >= 0) & (live > 0)

        def earlier_block(state):
            n, _ = state
            pv, total = weigh(logits(q_start, n, t), n, t, None)
            carry = carry_ref[slot]
            acc_ref[slot] = acc_ref[slot] + jnp.exp2(carry) * pv
            carry_ref[slot] = carry + total
            return n - 1, alive(slot)

        if first_block >= 0:
            lax.while_loop(more, earlier_block, (jnp.int32(first_block), live))
        acc = acc_ref[slot]
        o_ref[0, q_start:q_start + half, :] = _merge_heads(
            [acc[h * half:(h + 1) * half] for h in range(HEADS_PER_TILE)], masks).astype(o_ref.dtype)

    def tile_group(tiles):
        slabs = []
        pending = []
        for i, qi in enumerate(tiles):
            top = qi * t
            slabs.append((2 * i, top, qi, half, earlier_keys(half, 0)))
            if qi:
                slabs.append((2 * i, top, qi - 1, t, None))
            slabs.append((2 * i + 1, top + half, qi, t, earlier_keys(t, half)))
            pending += [(2 * i, top, qi - 2), (2 * i + 1, top + half, qi - 1)]
        z_next = logits(*slabs[0][1:4])
        started = set()
        for j, (slot, q_start, block, nk, mask) in enumerate(slabs):
            z = z_next
            if j + 1 < len(slabs):
                z_next = logits(*slabs[j + 1][1:4])
            pv, total = weigh(z, block, nk, mask)
            if slot in started:
                carry = carry_ref[slot]
                acc_ref[slot] = acc_ref[slot] + jnp.exp2(carry) * pv
                carry_ref[slot] = carry + total
            else:
                acc_ref[slot] = pv
                carry_ref[slot] = total
                started.add(slot)
        lives = [alive(slot) if first_block >= 0 else None for slot, _, first_block in pending]
        for (slot, q_start, first_block), live in zip(pending, lives):
            finish(slot, q_start, first_block, live)

    for first in range(0, n_tiles, SB_TILE_GROUP):
        tile_group(range(first, first + SB_TILE_GROUP))


def _out_kernel(x_ref, ym_ref, ys_ref, gate_ref, wum_ref, wus_ref, wo_ref, gpm_ref,
                gpre_ref, wmi_ref, wmo_ref, gpost_ref, o_ref):
    d = x_ref.shape[-1]
    ff = wmi_ref.shape[1]
    rows = x_ref.shape[0] // OUT_CHAINS
    chains = [pl.ds(c * rows, rows) for c in range(OUT_CHAINS)]

    def mix(r):
        up_m = _dot(ym_ref[r, :], wum_ref[...])
        up_s = _dot(ys_ref[r, :], wus_ref[...])
        mixed = gate_ref[r, :d].astype(F32) * up_m + gate_ref[r, d:].astype(F32) * up_s
        return _dot(mixed.astype(BF16), wo_ref[...])

    def mlp_chunk(hn, j):
        u = _dot(hn, wmi_ref[:, j * d:(j + 1) * d])
        u = jnp.square(jnp.maximum(u, 0.0))
        return _dot(u.astype(BF16), wmo_ref[j * d:(j + 1) * d, :])

    mix_out = [mix(r) for r in chains]
    x1, hn, acc = [], [], []
    for c, r in enumerate(chains):
        x1.append(x_ref[r, :] + _rms(mix_out[c], gpm_ref[...]))
        hn.append(_rms(x1[c], gpre_ref[...]).astype(BF16))
        acc.append(mlp_chunk(hn[c], 0))
    for j in range(1, ff // d):
        for c in range(OUT_CHAINS):
            acc[c] = acc[c] + mlp_chunk(hn[c], j)
    for c, r in enumerate(chains):
        o_ref[r, :] = x1[c] + _rms(acc[c], gpost_ref[...])


def _output_block(x2, ym, ys, gate, wum, wus, wo, gpm, gpre, wmi, wmo, gpost):
    n, d = x2.shape
    tm = OUT_TILE
    tok = lambda i: (i, 0)
    const = lambda i: (0, 0)
    resident = lambda a: pl.BlockSpec(a.shape, const, pipeline_mode=pl.Buffered(1))
    return pl.pallas_call(
        _out_kernel,
        grid=(n // tm,),
        in_specs=[
            pl.BlockSpec((tm, d), tok),
            pl.BlockSpec((tm, ym.shape[1]), tok),
            pl.BlockSpec((tm, ys.shape[1]), tok),
            pl.BlockSpec((tm, gate.shape[1]), tok),
            resident(wum), resident(wus), resident(wo), resident(gpm),
            resident(gpre), resident(wmi), resident(wmo), resident(gpost),
        ],
        out_specs=pl.BlockSpec((tm, d), tok),
        out_shape=jax.ShapeDtypeStruct((n, d), F32),
        compiler_params=pltpu.CompilerParams(
            dimension_semantics=("arbitrary",), vmem_limit_bytes=VMEM_LIMIT),
    )(x2, ym, ys, gate, wum, wus, wo, gpm, gpre, wmi, wmo, gpost)


def _rope_tables(seq):
    half = HEAD_DIM // 2
    inv_freq = ROPE_THETA ** (-jnp.arange(half, dtype=F32) * 2.0 / HEAD_DIM)
    ang = jnp.arange(seq, dtype=F32)[:, None] * inv_freq[None, :]
    ang = jnp.concatenate([ang] * (2 * HEADS_PER_TILE), axis=-1)
    cos, sin = jnp.cos(ang), jnp.sin(ang)
    first_half = (jnp.arange(LANES) % HEAD_DIM) < half
    return cos, jnp.where(first_half, -sin, 0.0), jnp.where(first_half, 0.0, sin)


def kernel(x, g_pre_mix, w_in, b_gate, w_up_moba, w_up_sb, w_out, g_post_mix,
           g_pre_mlp, w_mlp_in, w_mlp_out, g_post_mlp):
    b, s, d = x.shape
    depth = w_in.shape[0]
    cos, sina, sinb = _rope_tables(s)
    for l in range(depth):
        w = BRANCH_WIDTH
        wkbt = w_in[l][:, 4 * w:5 * w].T.astype(BF16)
        qa, kat, va, qb, kbt, vb, gate, kmean = _in_projection(
            x, g_pre_mix[l][None], w_in[l].astype(BF16), wkbt, b_gate[l][None], cos, sina, sinb)
        nkb = kmean.shape[1]
        y_moba = _attention_call(
            _moba_kernel, qa, kat, va, [kmean.reshape(b, nkb, -1)],
            [pl.BlockSpec((1, nkb, LANES), lambda bi, hp: (bi, 0, hp))],
            [pltpu.VMEM((HEADS_PER_TILE, LANES, s), BF16), pltpu.VMEM((HEADS_PER_TILE, s, LANES), BF16)])
        y_sb = _attention_call(
            _sb_kernel, qb, kbt, vb, [], [],
            [pltpu.VMEM((2 * SB_TILE_GROUP, KEY_BLOCK, LANES), F32)] * 2)
        x = _output_block(
            x.reshape(b * s, d), y_moba.reshape(b * s, -1), y_sb.reshape(b * s, -1),
            gate.reshape(b * s, -1),
            w_up_moba[l].astype(BF16), w_up_sb[l].astype(BF16), w_out[l].astype(BF16),
            g_post_mix[l][None], g_pre_mlp[l][None],
            w_mlp_in[l].astype(BF16), w_mlp_out[l].astype(BF16), g_post_mlp[l][None],
        ).reshape(b, s, d)
    return x
```

```python
import math

import jax
import jax.numpy as jnp
from jax import lax
from jax.experimental import pallas as pl
from jax.experimental.pallas import tpu as pltpu

F32 = jnp.float32
BF16 = jnp.bfloat16

HEAD_DIM = 64
N_HEADS = 8
BRANCH_WIDTH = N_HEADS * HEAD_DIM
KEY_BLOCK = 256
TOPK = 3
ROPE_THETA = 10000.0
RMS_EPS = 1e-6
NEG = -1e30
LANES = 128
SUBLANES = 8
HEADS_PER_TILE = LANES // HEAD_DIM
VMEM_LIMIT = 56 * 1024 * 1024

GATE_GROUP = 8
GATE_GROUP_SHIFT = 3
SB_DEAD_LOG2 = -126.0
LOG2_E = 1.4426950408889634
SB_TILE_GROUP = 8
MOBA_TILE_ORDER = (3, 2, 1, 0, 7, 6, 5, 4)

FEATURE_MAJOR_COLUMNS = (2, 4)

IN_TILE = 1024
OUT_TILE = 512
OUT_CHAINS = 2


def _dot(a, b):
    return jnp.dot(a, b, preferred_element_type=F32)


def _dot_nt(a, b):
    return lax.dot_general(a, b, (((1,), (1,)), ((), ())), preferred_element_type=F32)


def _rms(x, g):
    ms = jnp.mean(x * x, axis=-1, keepdims=True)
    return x * lax.rsqrt(ms + RMS_EPS) * g


def _inproj_kernel(x_ref, g_ref, w_ref, b_ref, cos_ref, sina_ref, sinb_ref,
                   qa_ref, kat_ref, vat_ref, qb_ref, kbt_ref, vb_ref, gate_ref, km_ref, wt_ref):
    w = BRANCH_WIDTH

    @pl.when((pl.program_id(0) == 0) & (pl.program_id(1) == 0))
    def _():
        for slot, c in enumerate(FEATURE_MAJOR_COLUMNS):
            wt_ref[slot] = w_ref[:, c * w:(c + 1) * w].astype(F32).T.astype(BF16)

    reps = w // LANES
    scale = LOG2_E / math.sqrt(HEAD_DIM)
    half = HEAD_DIM // 2
    blocks = [pl.ds(i * KEY_BLOCK, KEY_BLOCK) for i in range(x_ref.shape[1] // KEY_BLOCK)]
    hns = [_rms(x_ref[0, r, :], g_ref[...]).astype(BF16) for r in blocks]

    tables = [[jnp.concatenate([ref[r, :]] * reps, axis=1) for ref in (cos_ref, sina_ref, sinb_ref)]
              for r in blocks]

    def proj(hn, c):
        return _dot(hn, w_ref[:, c * w:(c + 1) * w])

    def rope(t, i):
        cos, sina, sinb = tables[i]
        return t * cos + pltpu.roll(t, w - half, 1) * sina + pltpu.roll(t, half, 1) * sinb

    chains = list(enumerate(zip(blocks, hns)))
    for i, (r, hn) in chains:
        qa_ref[0, r, :] = (rope(proj(hn, 0), i) * scale).astype(BF16)
    for i, (r, hn) in chains:
        ka = rope(proj(hn, 1), i)
        km_ref[0, i] = jnp.sum(ka, axis=0, keepdims=True) * (1.0 / KEY_BLOCK)
        kat_ref[0, i] = ka.T.astype(BF16)
    for i, (r, hn) in chains:
        vat_ref[0, i] = _dot_nt(wt_ref[0], hn).astype(BF16)
    for i, (r, hn) in chains:
        qb_ref[0, r, :] = (proj(hn, 3) * scale).astype(BF16)
    for i, (r, hn) in chains:
        kbt_ref[0, i] = _dot_nt(wt_ref[1], hn).astype(BF16)
    for i, (r, hn) in chains:
        vb_ref[0, r, :] = proj(hn, 5).astype(BF16)
    for i, (r, hn) in chains:
        gates = _dot(hn, w_ref[:, 6 * w:]) + b_ref[...]
        gate_ref[0, r, :] = jax.nn.sigmoid(gates).astype(BF16)


def _in_projection(x, g_pre, w_in, b_gate, cos, sina, sinb):
    b, s, d = x.shape
    tm = IN_TILE
    w = BRANCH_WIDTH
    gw = w_in.shape[1] - 6 * w
    nkb = s // KEY_BLOCK
    const = lambda i, j: (0, 0)
    tok = lambda i, j: (j, i, 0)
    tab = lambda i, j: (i, 0)
    branch = jax.ShapeDtypeStruct((b, s, w), BF16)
    branch_spec = pl.BlockSpec((1, tm, w), tok)
    feat_major = jax.ShapeDtypeStruct((b, nkb, w, KEY_BLOCK), BF16)
    feat_major_spec = pl.BlockSpec((1, tm // KEY_BLOCK, w, KEY_BLOCK), lambda i, j: (j, i, 0, 0))
    resident = lambda a: pl.BlockSpec(a.shape, const, pipeline_mode=pl.Buffered(1))
    return pl.pallas_call(
        _inproj_kernel,
        grid=(s // tm, b),
        in_specs=[
            pl.BlockSpec((1, tm, d), tok),
            pl.BlockSpec((1, d), const),
            resident(w_in),
            pl.BlockSpec((1, gw), const),
            pl.BlockSpec((tm, LANES), tab),
            pl.BlockSpec((tm, LANES), tab),
            pl.BlockSpec((tm, LANES), tab),
        ],
        out_specs=[branch_spec, feat_major_spec, feat_major_spec, branch_spec, feat_major_spec, branch_spec] + [
            pl.BlockSpec((1, tm, gw), tok),
            pl.BlockSpec((1, tm // KEY_BLOCK, 1, w), lambda i, j: (j, i, 0, 0)),
        ],
        out_shape=[branch, feat_major, feat_major, branch, feat_major, branch] + [
            jax.ShapeDtypeStruct((b, s, gw), BF16),
            jax.ShapeDtypeStruct((b, nkb, 1, w), F32),
        ],
        scratch_shapes=[pltpu.VMEM((len(FEATURE_MAJOR_COLUMNS), w, d), BF16)],
        compiler_params=pltpu.CompilerParams(
            dimension_semantics=("arbitrary", "arbitrary"), vmem_limit_bytes=VMEM_LIMIT),
    )(x, g_pre, w_in, b_gate, cos, sina, sinb)


def _head_masks(shape):
    lane = lax.broadcasted_iota(jnp.int32, shape, len(shape) - 1)
    return [(lane >= h * HEAD_DIM) & (lane < (h + 1) * HEAD_DIM) for h in range(HEADS_PER_TILE)]


def _merge_heads(outs, masks):
    y = outs[0]
    for o, m in zip(outs[1:], masks[1:]):
        y = jnp.where(m, o, y)
    return y


def _gate_lane_offset(h):
    return HEAD_DIM * (HEADS_PER_TILE - 1 - h)


def _moba_bias(q_ref, km_ref, t, n_tiles):
    lane = lax.broadcasted_iota(jnp.int32, (t, LANES), 1)
    masks = _head_masks((n_tiles, LANES))
    km = km_ref[0]
    rows_hi, rows_lo = [], []
    for i in range(n_tiles):
        parts_hi, parts_lo = [], []
        for h in reversed(range(HEADS_PER_TILE)):
            kmh = jnp.where(masks[h], km, 0.0)
            hi = kmh.astype(BF16)
            lo = (kmh - hi.astype(F32)).astype(BF16)
            before = GATE_GROUP * i
            after = HEAD_DIM - before - n_tiles
            for parts, val in ((parts_hi, hi), (parts_lo, lo)):
                if before:
                    parts.append(jnp.zeros((before, LANES), BF16))
                parts.append(val)
                if after:
                    parts.append(jnp.zeros((after, LANES), BF16))
        rows_hi.append(jnp.concatenate(parts_hi, axis=0))
        rows_lo.append(jnp.concatenate(parts_lo, axis=0))
    w_hi = jnp.concatenate(rows_hi, axis=1)
    w_lo = jnp.concatenate(rows_lo, axis=1)
    q_all = jnp.concatenate([q_ref[0, i * t:(i + 1) * t, :] for i in range(n_tiles)], axis=1)
    gate = _dot_nt(q_all, w_hi) + _dot_nt(q_all, w_lo)
    yield None

    blk = lane & (GATE_GROUP - 1)
    tile = (lane >> GATE_GROUP_SHIFT) & (GATE_GROUP - 1)
    beaten = jnp.zeros((t, LANES), F32)
    for d in range(1, GATE_GROUP):
        wraps = blk + d >= GATE_GROUP
        other = jnp.where(wraps, pltpu.roll(gate, GATE_GROUP - d, 1), pltpu.roll(gate, LANES - d, 1))
        other_blk = jnp.where(wraps, blk + d - GATE_GROUP, blk + d)
        wins = (other > gate) | ((other == gate) & wraps)
        beaten = beaten + jnp.where(wins & (other_blk < tile), 1.0, 0.0)
        yield None
    dropped = (blk < tile) & (beaten >= TOPK)
    yield jnp.where(dropped, NEG, 0.0)


def _moba_kernel(q_ref, kt_ref, vt_ref, km_ref, o_ref, kaug_ref, vaug_ref):
    t = KEY_BLOCK
    n_tiles = q_ref.shape[1] // t
    assert n_tiles == GATE_GROUP
    lane = lax.broadcasted_iota(jnp.int32, (t, LANES), 1)
    masks = _head_masks((t, LANES))
    row = lax.broadcasted_iota(jnp.int32, (t, t), 0)
    col = lax.broadcasted_iota(jnp.int32, (t, t), 1)
    causal = col <= row

    feat = lax.broadcasted_iota(jnp.int32, (LANES, t), 0)
    for h in range(HEADS_PER_TILE):
        off = _gate_lane_offset(h)
        in_head = (feat >= h * HEAD_DIM) & (feat < (h + 1) * HEAD_DIM)
        for n in range(n_tiles):
            onehot = jnp.where(feat == off + n, 1.0, 0.0).astype(BF16)
            kaug_ref[h, :, n * t:(n + 1) * t] = jnp.where(in_head, kt_ref[0, n], onehot)

    ones_row = jnp.where(lax.broadcasted_iota(jnp.int32, (SUBLANES, t), 0) == 0, 1.0, 0.0).astype(BF16)
    for h in range(HEADS_PER_TILE):
        for n in range(n_tiles):
            vaug_ref[h, :, n * t:(n + 1) * t] = jnp.concatenate(
                [vt_ref[0, n, h * HEAD_DIM:(h + 1) * HEAD_DIM, :], ones_row], axis=0)

    bias_steps = _moba_bias(q_ref, km_ref, t, n_tiles)
    bias_all = []

    def scores(qi, h):
        q2 = q_ref[0, qi * t:(qi + 1) * t, :]
        qh = jnp.where(masks[h], q2, jnp.zeros_like(q2))
        if qi > TOPK:
            bias_all.extend(step for step in bias_steps if step is not None)
            off = _gate_lane_offset(h)
            bias_qi = pltpu.roll(bias_all[0], LANES - GATE_GROUP * qi, 1)
            in_group = (lane >= off) & (lane < off + GATE_GROUP)
            qh = (qh.astype(F32) + jnp.where(in_group, bias_qi, 0.0)).astype(BF16)
        return _dot(qh, kaug_ref[h, :, 0:(qi + 1) * t])

    def attend(qi, h, s):
        nk = (qi + 1) * t
        s_own = jnp.where(causal, s[:, qi * t:], NEG)
        m = jnp.max(s_own, axis=-1, keepdims=True)
        if qi:
            s_past = s[:, :qi * t]
            m = jnp.maximum(m, jnp.max(s_past, axis=-1, keepdims=True))
            p = jnp.concatenate([jnp.exp2((s_past - m).astype(BF16)),
                                 jnp.exp2((s_own - m).astype(BF16))], axis=1)
        else:
            p = jnp.exp2((s_own - m).astype(BF16))
        num_den = _dot_nt(vaug_ref[h, :, 0:nk], p)
        return (num_den[:HEAD_DIM] / num_den[HEAD_DIM:HEAD_DIM + 1]).astype(o_ref.dtype)

    chains = [(qi, h) for qi in MOBA_TILE_ORDER for h in range(HEADS_PER_TILE)]
    s_next = scores(*chains[0])
    for i, (qi, h) in enumerate(chains):
        s = s_next
        if i + 1 < len(chains):
            s_next = scores(*chains[i + 1])
        o_ref[0, qi, h * HEAD_DIM:(h + 1) * HEAD_DIM, :] = attend(qi, h, s)
        if not bias_all:
            step = next(bias_steps)
            if step is not None:
                bias_all.append(step)


def _attention_call(kernel, q, kt, v, v_feature_major, extra_inputs, extra_specs, scratch_shapes):
    b, s, w = q.shape
    spec = pl.BlockSpec((1, s, LANES), lambda bi, hp: (bi, 0, hp))
    kt_spec = pl.BlockSpec((1, s // KEY_BLOCK, LANES, KEY_BLOCK), lambda bi, hp: (bi, 0, hp, 0))
    v_spec = kt_spec if v_feature_major else spec
    return pl.pallas_call(
        kernel,
        grid=(b, w // LANES),
        in_specs=[spec, kt_spec, v_spec] + extra_specs,
        out_specs=v_spec,
        out_shape=jax.ShapeDtypeStruct(v.shape, BF16),
        scratch_shapes=scratch_shapes,
        compiler_params=pltpu.CompilerParams(
            dimension_semantics=("arbitrary", "arbitrary"), vmem_limit_bytes=VMEM_LIMIT),
    )(q, kt, v, *extra_inputs)


def _sb_kernel(q_ref, kt_ref, v_ref, o_ref, acc_ref, carry_ref):
    t = KEY_BLOCK
    half = t // 2
    rows = HEADS_PER_TILE * half
    n_tiles = q_ref.shape[1] // t
    masks = _head_masks((half, LANES))
    tri = jnp.where(lax.broadcasted_iota(jnp.int32, (t, t), 0) > lax.broadcasted_iota(jnp.int32, (t, t), 1),
                    1.0, 0.0).astype(BF16)

    def earlier_keys(nk, first_query):
        q_pos = (lax.broadcasted_iota(jnp.int32, (rows, nk), 0) & (half - 1)) + first_query
        return lax.broadcasted_iota(jnp.int32, (rows, nk), 1) < q_pos

    def logits(q_start, block, nk):
        q2 = q_ref[0, pl.ds(q_start, half), :]
        qs = jnp.concatenate([jnp.where(m, q2, jnp.zeros_like(q2)) for m in masks], axis=0)
        return _dot(qs, kt_ref[0, block, :, 0:nk])

    def weigh(z, block, nk, earlier):
        v2 = v_ref[0, pl.ds(pl.multiple_of(block * t, t), nk), :]
        log_beta = jnp.minimum(z, 0.0) - jnp.log2(1.0 + jnp.exp2(-jnp.abs(z)))
        log_1mb = log_beta - z
        if earlier is not None:
            log_1mb = jnp.where(earlier, log_1mb, 0.0)
        suffix = _dot(log_1mb.astype(BF16), tri[:nk, :nk])
        a = jnp.exp2(log_beta + suffix)
        if earlier is not None:
            a = jnp.where(earlier, a, 0.0)
        total = suffix[:, 0:1] + log_1mb[:, 0:1]
        return _dot(a.astype(BF16), v2), jnp.broadcast_to(total, (rows, LANES))

    def alive(slot):
        return (jnp.max(carry_ref[slot]) > SB_DEAD_LOG2).astype(jnp.int32)

    def finish(slot, q_start, first_block, live):
        def more(state):
            n, live = state
            return (n >= 0) & (live > 0)

        def earlier_block(state):
            n, _ = state
            pv, total = weigh(logits(q_start, n, t), n, t, None)
            carry = carry_ref[slot]
            acc_ref[slot] = acc_ref[slot] + jnp.exp2(carry) * pv
            carry_ref[slot] = carry + total
            return n - 1, alive(slot)

        if first_block >= 0:
            lax.while_loop(more, earlier_block, (jnp.int32(first_block), live))
        acc = acc_ref[slot]
        o_ref[0, q_start:q_start + half, :] = _merge_heads(
            [acc[h * half:(h + 1) * half] for h in range(HEADS_PER_TILE)], masks).astype(o_ref.dtype)

    def tile_group(tiles):
        slabs = []
        pending = []
        for i, qi in enumerate(tiles):
            top = qi * t
            slabs.append((2 * i, top, qi, half, earlier_keys(half, 0)))
            if qi:
                slabs.append((2 * i, top, qi - 1, t, None))
            slabs.append((2 * i + 1, top + half, qi, t, earlier_keys(t, half)))
            pending += [(2 * i, top, qi - 2), (2 * i + 1, top + half, qi - 1)]
        z_next = logits(*slabs[0][1:4])
        started = set()
        for j, (slot, q_start, block, nk, mask) in enumerate(slabs):
            z = z_next
            if j + 1 < len(slabs):
                z_next = logits(*slabs[j + 1][1:4])
            pv, total = weigh(z, block, nk, mask)
            if slot in started:
                carry = carry_ref[slot]
                acc_ref[slot] = acc_ref[slot] + jnp.exp2(carry) * pv
                carry_ref[slot] = carry + total
            else:
                acc_ref[slot] = pv
                carry_ref[slot] = total
                started.add(slot)
        lives = [alive(slot) if first_block >= 0 else None for slot, _, first_block in pending]
        for (slot, q_start, first_block), live in zip(pending, lives):
            finish(slot, q_start, first_block, live)

    for first in range(0, n_tiles, SB_TILE_GROUP):
        tile_group(range(first, first + SB_TILE_GROUP))


def _out_kernel(x_ref, ym_ref, ys_ref, gate_ref, wum_ref, wus_ref, wo_ref, gpm_ref,
                gpre_ref, wmi_ref, wmo_ref, gpost_ref, o_ref):
    d = x_ref.shape[-1]
    ff = wmi_ref.shape[1]
    rows = x_ref.shape[0] // OUT_CHAINS
    assert rows == KEY_BLOCK
    chains = [pl.ds(c * rows, rows) for c in range(OUT_CHAINS)]

    def mix(c, r):
        up_m = lax.dot_general(ym_ref[c], wum_ref[...], (((0,), (0,)), ((), ())), preferred_element_type=F32)
        up_s = _dot(ys_ref[r, :], wus_ref[...])
        mixed = gate_ref[r, :d].astype(F32) * up_m + gate_ref[r, d:].astype(F32) * up_s
        return _dot(mixed.astype(BF16), wo_ref[...])

    def mlp_chunk(hn, j):
        u = _dot(hn, wmi_ref[:, j * d:(j + 1) * d])
        u = jnp.square(jnp.maximum(u, 0.0))
        return _dot(u.astype(BF16), wmo_ref[j * d:(j + 1) * d, :])

    mix_out = [mix(c, r) for c, r in enumerate(chains)]
    x1, hn, acc = [], [], []
    for c, r in enumerate(chains):
        x1.append(x_ref[r, :] + _rms(mix_out[c], gpm_ref[...]))
        hn.append(_rms(x1[c], gpre_ref[...]).astype(BF16))
        acc.append(mlp_chunk(hn[c], 0))
    for j in range(1, ff // d):
        for c in range(OUT_CHAINS):
            acc[c] = acc[c] + mlp_chunk(hn[c], j)
    for c, r in enumerate(chains):
        o_ref[r, :] = x1[c] + _rms(acc[c], gpost_ref[...])


def _output_block(x2, ym, ys, gate, wum, wus, wo, gpm, gpre, wmi, wmo, gpost):
    n, d = x2.shape
    tm = OUT_TILE
    tok = lambda i: (i, 0)
    const = lambda i: (0, 0)
    resident = lambda a: pl.BlockSpec(a.shape, const, pipeline_mode=pl.Buffered(1))
    return pl.pallas_call(
        _out_kernel,
        grid=(n // tm,),
        in_specs=[
            pl.BlockSpec((tm, d), tok),
            pl.BlockSpec((tm // KEY_BLOCK,) + ym.shape[1:], lambda i: (i, 0, 0)),
            pl.BlockSpec((tm, ys.shape[1]), tok),
            pl.BlockSpec((tm, gate.shape[1]), tok),
            resident(wum), resident(wus), resident(wo), resident(gpm),
            resident(gpre), resident(wmi), resident(wmo), resident(gpost),
        ],
        out_specs=pl.BlockSpec((tm, d), tok),
        out_shape=jax.ShapeDtypeStruct((n, d), F32),
        compiler_params=pltpu.CompilerParams(
            dimension_semantics=("arbitrary",), vmem_limit_bytes=VMEM_LIMIT),
    )(x2, ym, ys, gate, wum, wus, wo, gpm, gpre, wmi, wmo, gpost)


def _rope_tables(seq):
    half = HEAD_DIM // 2
    inv_freq = ROPE_THETA ** (-jnp.arange(half, dtype=F32) * 2.0 / HEAD_DIM)
    ang = jnp.arange(seq, dtype=F32)[:, None] * inv_freq[None, :]
    ang = jnp.concatenate([ang] * (2 * HEADS_PER_TILE), axis=-1)
    cos, sin = jnp.cos(ang), jnp.sin(ang)
    first_half = (jnp.arange(LANES) % HEAD_DIM) < half
    return cos, jnp.where(first_half, -sin, 0.0), jnp.where(first_half, 0.0, sin)


def kernel(x, g_pre_mix, w_in, b_gate, w_up_moba, w_up_sb, w_out, g_post_mix,
           g_pre_mlp, w_mlp_in, w_mlp_out, g_post_mlp):
    b, s, d = x.shape
    depth = w_in.shape[0]
    cos, sina, sinb = _rope_tables(s)
    for l in range(depth):
        w = BRANCH_WIDTH
        qa, kat, vat, qb, kbt, vb, gate, kmean = _in_projection(
            x, g_pre_mix[l][None], w_in[l].astype(BF16), b_gate[l][None], cos, sina, sinb)
        nkb = kmean.shape[1]
        y_moba = _attention_call(
            _moba_kernel, qa, kat, vat, True, [kmean.reshape(b, nkb, -1)],
            [pl.BlockSpec((1, nkb, LANES), lambda bi, hp: (bi, 0, hp))],
            [pltpu.VMEM((HEADS_PER_TILE, LANES, s), BF16),
             pltpu.VMEM((HEADS_PER_TILE, HEAD_DIM + SUBLANES, s), BF16)])
        y_sb = _attention_call(
            _sb_kernel, qb, kbt, vb, False, [], [],
            [pltpu.VMEM((2 * SB_TILE_GROUP, KEY_BLOCK, LANES), F32)] * 2)
        x = _output_block(
            x.reshape(b * s, d), y_moba.reshape(b * nkb, w, -1), y_sb.reshape(b * s, -1),
            gate.reshape(b * s, -1),
            w_up_moba[l].astype(BF16), w_up_sb[l].astype(BF16), w_out[l].astype(BF16),
            g_post_mix[l][None], g_pre_mlp[l][None],
            w_mlp_in[l].astype(BF16), w_mlp_out[l].astype(BF16), g_post_mlp[l][None],
        ).reshape(b, s, d)
    return x
```

```python
import math

import jax
import jax.numpy as jnp
from jax import lax
from jax.experimental import pallas as pl
from jax.experimental.pallas import tpu as pltpu

F32 = jnp.float32
BF16 = jnp.bfloat16

HEAD_DIM = 64
N_HEADS = 8
BRANCH_WIDTH = N_HEADS * HEAD_DIM
KEY_BLOCK = 256
TOPK = 3
ROPE_THETA = 10000.0
RMS_EPS = 1e-6
NEG = -1e30
LANES = 128
SUBLANES = 8
HEADS_PER_TILE = LANES // HEAD_DIM
VMEM_LIMIT = 56 * 1024 * 1024

GATE_GROUP = 8
GATE_GROUP_SHIFT = 3
SB_DEAD_LOG2 = -126.0
LOG2_E = 1.4426950408889634
SB_TILE_GROUP = 8
MOBA_TILE_ORDER = (3, 2, 1, 0, 7, 6, 5, 4)

FEATURE_MAJOR_COLUMNS = (2, 4)

IN_TILE = 1024
OUT_TILE = 512
OUT_CHAINS = 2


def _dot(a, b):
    return jnp.dot(a, b, preferred_element_type=F32)


def _dot_nt(a, b):
    return lax.dot_general(a, b, (((1,), (1,)), ((), ())), preferred_element_type=F32)


def _rms(x, g):
    ms = jnp.mean(x * x, axis=-1, keepdims=True)
    return x * lax.rsqrt(ms + RMS_EPS) * g


def _inproj_kernel(x_ref, g_ref, w_ref, b_ref, cos_ref, sina_ref, sinb_ref,
                   qa_ref, kat_ref, vat_ref, qb_ref, kbt_ref, vb_ref, gate_ref, km_ref, wt_ref):
    w = BRANCH_WIDTH

    @pl.when((pl.program_id(0) == 0) & (pl.program_id(1) == 0))
    def _():
        for slot, c in enumerate(FEATURE_MAJOR_COLUMNS):
            wt_ref[slot] = w_ref[:, c * w:(c + 1) * w].astype(F32).T.astype(BF16)

    reps = w // LANES
    scale = LOG2_E / math.sqrt(HEAD_DIM)
    half = HEAD_DIM // 2
    blocks = [pl.ds(i * KEY_BLOCK, KEY_BLOCK) for i in range(x_ref.shape[1] // KEY_BLOCK)]
    hns = [_rms(x_ref[0, r, :], g_ref[...]).astype(BF16) for r in blocks]

    tables = [[jnp.concatenate([ref[r, :]] * reps, axis=1) for ref in (cos_ref, sina_ref, sinb_ref)]
              for r in blocks]

    def proj(hn, c):
        return _dot(hn, w_ref[:, c * w:(c + 1) * w])

    def rope(t, i):
        cos, sina, sinb = tables[i]
        return t * cos + pltpu.roll(t, w - half, 1) * sina + pltpu.roll(t, half, 1) * sinb

    chains = list(enumerate(zip(blocks, hns)))
    for i, (r, hn) in chains:
        qa_ref[0, r, :] = (rope(proj(hn, 0), i) * scale).astype(BF16)
    for i, (r, hn) in chains:
        ka = rope(proj(hn, 1), i)
        km_ref[0, i] = jnp.sum(ka, axis=0, keepdims=True) * (1.0 / KEY_BLOCK)
        kat_ref[0, i] = ka.T.astype(BF16)
    for i, (r, hn) in chains:
        vat_ref[0, i] = _dot_nt(wt_ref[0], hn).astype(BF16)
    for i, (r, hn) in chains:
        qb_ref[0, r, :] = (proj(hn, 3) * scale).astype(BF16)
    for i, (r, hn) in chains:
        kbt_ref[0, i] = _dot_nt(wt_ref[1], hn).astype(BF16)
    for i, (r, hn) in chains:
        vb_ref[0, r, :] = proj(hn, 5).astype(BF16)
    for i, (r, hn) in chains:
        gates = _dot(hn, w_ref[:, 6 * w:]) + b_ref[...]
        gate_ref[0, r, :] = jax.nn.sigmoid(gates).astype(BF16)


def _in_projection(x, g_pre, w_in, b_gate, cos, sina, sinb):
    b, s, d = x.shape
    tm = IN_TILE
    w = BRANCH_WIDTH
    gw = w_in.shape[1] - 6 * w
    nkb = s // KEY_BLOCK
    const = lambda i, j: (0, 0)
    tok = lambda i, j: (j, i, 0)
    tab = lambda i, j: (i, 0)
    branch = jax.ShapeDtypeStruct((b, s, w), BF16)
    branch_spec = pl.BlockSpec((1, tm, w), tok)
    feat_major = jax.ShapeDtypeStruct((b, nkb, w, KEY_BLOCK), BF16)
    feat_major_spec = pl.BlockSpec((1, tm // KEY_BLOCK, w, KEY_BLOCK), lambda i, j: (j, i, 0, 0))
    resident = lambda a: pl.BlockSpec(a.shape, const, pipeline_mode=pl.Buffered(1))
    return pl.pallas_call(
        _inproj_kernel,
        grid=(s // tm, b),
        in_specs=[
            pl.BlockSpec((1, tm, d), tok),
            pl.BlockSpec((1, d), const),
            resident(w_in),
            pl.BlockSpec((1, gw), const),
            pl.BlockSpec((tm, LANES), tab),
            pl.BlockSpec((tm, LANES), tab),
            pl.BlockSpec((tm, LANES), tab),
        ],
        out_specs=[branch_spec, feat_major_spec, feat_major_spec, branch_spec, feat_major_spec, branch_spec] + [
            pl.BlockSpec((1, tm, gw), tok),
            pl.BlockSpec((1, tm // KEY_BLOCK, 1, w), lambda i, j: (j, i, 0, 0)),
        ],
        out_shape=[branch, feat_major, feat_major, branch, feat_major, branch] + [
            jax.ShapeDtypeStruct((b, s, gw), BF16),
            jax.ShapeDtypeStruct((b, nkb, 1, w), F32),
        ],
        scratch_shapes=[pltpu.VMEM((len(FEATURE_MAJOR_COLUMNS), w, d), BF16)],
        compiler_params=pltpu.CompilerParams(
            dimension_semantics=("arbitrary", "arbitrary"), vmem_limit_bytes=VMEM_LIMIT),
    )(x, g_pre, w_in, b_gate, cos, sina, sinb)


def _head_masks(shape):
    lane = lax.broadcasted_iota(jnp.int32, shape, len(shape) - 1)
    return [(lane >= h * HEAD_DIM) & (lane < (h + 1) * HEAD_DIM) for h in range(HEADS_PER_TILE)]


def _merge_heads(outs, masks):
    y = outs[0]
    for o, m in zip(outs[1:], masks[1:]):
        y = jnp.where(m, o, y)
    return y


def _gate_lane_offset(h):
    return HEAD_DIM * (HEADS_PER_TILE - 1 - h)


def _moba_bias(q_ref, km_ref, t, n_tiles):
    lane = lax.broadcasted_iota(jnp.int32, (t, LANES), 1)
    masks = _head_masks((n_tiles, LANES))
    km = km_ref[0]
    rows_hi, rows_lo = [], []
    for i in range(n_tiles):
        parts_hi, parts_lo = [], []
        for h in reversed(range(HEADS_PER_TILE)):
            kmh = jnp.where(masks[h], km, 0.0)
            hi = kmh.astype(BF16)
            lo = (kmh - hi.astype(F32)).astype(BF16)
            before = GATE_GROUP * i
            after = HEAD_DIM - before - n_tiles
            for parts, val in ((parts_hi, hi), (parts_lo, lo)):
                if before:
                    parts.append(jnp.zeros((before, LANES), BF16))
                parts.append(val)
                if after:
                    parts.append(jnp.zeros((after, LANES), BF16))
        rows_hi.append(jnp.concatenate(parts_hi, axis=0))
        rows_lo.append(jnp.concatenate(parts_lo, axis=0))
    w_hi = jnp.concatenate(rows_hi, axis=1)
    w_lo = jnp.concatenate(rows_lo, axis=1)
    q_all = jnp.concatenate([q_ref[0, i * t:(i + 1) * t, :] for i in range(n_tiles)], axis=1)
    gate = _dot_nt(q_all, w_hi) + _dot_nt(q_all, w_lo)
    yield None

    blk = lane & (GATE_GROUP - 1)
    tile = (lane >> GATE_GROUP_SHIFT) & (GATE_GROUP - 1)
    beaten = jnp.zeros((t, LANES), F32)
    for d in range(1, GATE_GROUP):
        wraps = blk + d >= GATE_GROUP
        other = jnp.where(wraps, pltpu.roll(gate, GATE_GROUP - d, 1), pltpu.roll(gate, LANES - d, 1))
        other_blk = jnp.where(wraps, blk + d - GATE_GROUP, blk + d)
        wins = (other > gate) | ((other == gate) & wraps)
        beaten = beaten + jnp.where(wins & (other_blk < tile), 1.0, 0.0)
        yield None
    dropped = (blk < tile) & (beaten >= TOPK)
    yield jnp.where(dropped, NEG, 0.0)


def _moba_kernel(q_ref, kt_ref, vt_ref, km_ref, o_ref, kaug_ref, vaug_ref):
    t = KEY_BLOCK
    n_tiles = q_ref.shape[1] // t
    assert n_tiles == GATE_GROUP
    lane = lax.broadcasted_iota(jnp.int32, (t, LANES), 1)
    masks = _head_masks((t, LANES))
    row = lax.broadcasted_iota(jnp.int32, (t, t), 0)
    col = lax.broadcasted_iota(jnp.int32, (t, t), 1)
    causal = col <= row

    feat = lax.broadcasted_iota(jnp.int32, (LANES, t), 0)
    for h in range(HEADS_PER_TILE):
        off = _gate_lane_offset(h)
        in_head = (feat >= h * HEAD_DIM) & (feat < (h + 1) * HEAD_DIM)
        for n in range(n_tiles):
            onehot = jnp.where(feat == off + n, 1.0, 0.0).astype(BF16)
            kaug_ref[h, :, n * t:(n + 1) * t] = jnp.where(in_head, kt_ref[0, n], onehot)

    ones_row = jnp.where(lax.broadcasted_iota(jnp.int32, (SUBLANES, t), 0) == 0, 1.0, 0.0).astype(BF16)
    for h in range(HEADS_PER_TILE):
        for n in range(n_tiles):
            vaug_ref[h, :, n * t:(n + 1) * t] = jnp.concatenate(
                [vt_ref[0, n, h * HEAD_DIM:(h + 1) * HEAD_DIM, :], ones_row], axis=0)

    bias_steps = _moba_bias(q_ref, km_ref, t, n_tiles)
    bias_all = []

    def scores(qi, h):
        q2 = q_ref[0, qi * t:(qi + 1) * t, :]
        qh = jnp.where(masks[h], q2, jnp.zeros_like(q2))
        if qi > TOPK:
            bias_all.extend(step for step in bias_steps if step is not None)
            off = _gate_lane_offset(h)
            bias_qi = pltpu.roll(bias_all[0], LANES - GATE_GROUP * qi, 1)
            in_group = (lane >= off) & (lane < off + GATE_GROUP)
            qh = (qh.astype(F32) + jnp.where(in_group, bias_qi, 0.0)).astype(BF16)
        return _dot(qh, kaug_ref[h, :, 0:(qi + 1) * t])

    def attend(qi, h, s):
        nk = (qi + 1) * t
        s_own = jnp.where(causal, s[:, qi * t:], NEG)
        m = jnp.max(s_own, axis=-1, keepdims=True)
        if qi:
            s_past = s[:, :qi * t]
            m = jnp.maximum(m, jnp.max(s_past, axis=-1, keepdims=True))
            p = jnp.concatenate([jnp.exp2((s_past - m).astype(BF16)),
                                 jnp.exp2((s_own - m).astype(BF16))], axis=1)
        else:
            p = jnp.exp2((s_own - m).astype(BF16))
        num_den = _dot_nt(vaug_ref[h, :, 0:nk], p)
        return (num_den[:HEAD_DIM] / num_den[HEAD_DIM:HEAD_DIM + 1]).astype(o_ref.dtype)

    chains = [(qi, h) for qi in MOBA_TILE_ORDER for h in range(HEADS_PER_TILE)]
    s_next = scores(*chains[0])
    for i, (qi, h) in enumerate(chains):
        s = s_next
        if i + 1 < len(chains):
            s_next = scores(*chains[i + 1])
        o_ref[0, qi, h * HEAD_DIM:(h + 1) * HEAD_DIM, :] = attend(qi, h, s)
        if not bias_all:
            step = next(bias_steps)
            if step is not None:
                bias_all.append(step)


def _attention_call(kernel, q, kt, v, v_feature_major, extra_inputs, extra_specs, scratch_shapes):
    b, s, w = q.shape
    spec = pl.BlockSpec((1, s, LANES), lambda bi, hp: (bi, 0, hp))
    kt_spec = pl.BlockSpec((1, s // KEY_BLOCK, LANES, KEY_BLOCK), lambda bi, hp: (bi, 0, hp, 0))
    v_spec = kt_spec if v_feature_major else spec
    return pl.pallas_call(
        kernel,
        grid=(b, w // LANES),
        in_specs=[spec, kt_spec, v_spec] + extra_specs,
        out_specs=v_spec,
        out_shape=jax.ShapeDtypeStruct(v.shape, BF16),
        scratch_shapes=scratch_shapes,
        compiler_params=pltpu.CompilerParams(
            dimension_semantics=("arbitrary", "arbitrary"), vmem_limit_bytes=VMEM_LIMIT),
    )(q, kt, v, *extra_inputs)


def _sb_kernel(q_ref, kt_ref, v_ref, o_ref, acc_ref, carry_ref):
    t = KEY_BLOCK
    half = t // 2
    rows = HEADS_PER_TILE * half
    n_tiles = q_ref.shape[1] // t
    masks = _head_masks((half, LANES))
    tri = jnp.where(lax.broadcasted_iota(jnp.int32, (t, t), 0) > lax.broadcasted_iota(jnp.int32, (t, t), 1),
                    1.0, 0.0).astype(BF16)

    def earlier_keys(nk, first_query):
        q_pos = (lax.broadcasted_iota(jnp.int32, (rows, nk), 0) & (half - 1)) + first_query
        return lax.broadcasted_iota(jnp.int32, (rows, nk), 1) < q_pos

    def logits(q_start, block, nk):
        q2 = q_ref[0, pl.ds(q_start, half), :]
        qs = jnp.concatenate([jnp.where(m, q2, jnp.zeros_like(q2)) for m in masks], axis=0)
        return _dot(qs, kt_ref[0, block, :, 0:nk])

    def decay(z, nk, earlier):
        if earlier is not None:
            z = jnp.where(earlier, z, NEG)
        log_beta = jnp.minimum(z, 0.0) - jnp.log2(1.0 + jnp.exp2(-jnp.abs(z)))
        log_1mb = log_beta - z
        suffix = _dot(log_1mb.astype(BF16), tri[:nk, :nk])
        total = suffix[:, 0:1] + log_1mb[:, 0:1]
        return log_beta, suffix, jnp.broadcast_to(total, (rows, LANES))

    def weighted_values(log_beta, suffix, block, nk):
        v2 = v_ref[0, pl.ds(pl.multiple_of(block * t, t), nk), :]
        return _dot(jnp.exp2(log_beta + suffix).astype(BF16), v2)

    def weigh(z, block, nk, earlier):
        log_beta, suffix, total = decay(z, nk, earlier)
        return weighted_values(log_beta, suffix, block, nk), total

    def alive(slot):
        return (jnp.max(carry_ref[slot]) > SB_DEAD_LOG2).astype(jnp.int32)

    def finish(slot, q_start, first_block, live):
        def more(state):
            n, live = state
            return (n >= 0) & (live > 0)

        def earlier_block(state):
            n, _ = state
            pv, total = weigh(logits(q_start, n, t), n, t, None)
            carry = carry_ref[slot]
            acc_ref[slot] = acc_ref[slot] + jnp.exp2(carry) * pv
            carry_ref[slot] = carry + total
            return n - 1, alive(slot)

        if first_block >= 0:
            lax.while_loop(more, earlier_block, (jnp.int32(first_block), live))
        acc = acc_ref[slot]
        o_ref[0, q_start:q_start + half, :] = _merge_heads(
            [acc[h * half:(h + 1) * half] for h in range(HEADS_PER_TILE)], masks).astype(o_ref.dtype)

    def tile_group(tiles):
        slabs = []
        pending = []
        for i, qi in enumerate(tiles):
            top = qi * t
            slabs.append((2 * i, top, qi, half, earlier_keys(half, 0)))
            if qi:
                slabs.append((2 * i, top, qi - 1, t, None))
            slabs.append((2 * i + 1, top + half, qi, t, earlier_keys(t, half)))
            pending += [(2 * i, top, qi - 2), (2 * i + 1, top + half, qi - 1)]
        n = len(slabs)
        z = {j: logits(*slabs[j][1:4]) for j in range(min(2, n))}
        mid = {0: decay(z.pop(0), slabs[0][3], slabs[0][4])}
        started = set()
        for j, (slot, q_start, block, nk, mask) in enumerate(slabs):
            if j + 2 < n:
                z[j + 2] = logits(*slabs[j + 2][1:4])
            if j + 1 < n:
                mid[j + 1] = decay(z.pop(j + 1), slabs[j + 1][3], slabs[j + 1][4])
            log_beta, suffix, total = mid.pop(j)
            pv = weighted_values(log_beta, suffix, block, nk)
            if slot in started:
                carry = carry_ref[slot]
                acc_ref[slot] = acc_ref[slot] + jnp.exp2(carry) * pv
                carry_ref[slot] = carry + total
            else:
                acc_ref[slot] = pv
                carry_ref[slot] = total
                started.add(slot)
        lives = [alive(slot) if first_block >= 0 else None for slot, _, first_block in pending]
        for (slot, q_start, first_block), live in zip(pending, lives):
            finish(slot, q_start, first_block, live)

    for first in range(0, n_tiles, SB_TILE_GROUP):
        tile_group(range(first, first + SB_TILE_GROUP))


def _out_kernel(x_ref, ym_ref, ys_ref, gate_ref, wum_ref, wus_ref, wo_ref, gpm_ref,
                gpre_ref, wmi_ref, wmo_ref, gpost_ref, o_ref):
    d = x_ref.shape[-1]
    ff = wmi_ref.shape[1]
    rows = x_ref.shape[0] // OUT_CHAINS
    assert rows == KEY_BLOCK
    chains = [pl.ds(c * rows, rows) for c in range(OUT_CHAINS)]

    def mix(c, r):
        up_m = lax.dot_general(ym_ref[c], wum_ref[...], (((0,), (0,)), ((), ())), preferred_element_type=F32)
        up_s = _dot(ys_ref[r, :], wus_ref[...])
        mixed = gate_ref[r, :d].astype(F32) * up_m + gate_ref[r, d:].astype(F32) * up_s
        return _dot(mixed.astype(BF16), wo_ref[...])

    def mlp_chunk(hn, j):
        u = _dot(hn, wmi_ref[:, j * d:(j + 1) * d])
        u = jnp.square(jnp.maximum(u, 0.0))
        return _dot(u.astype(BF16), wmo_ref[j * d:(j + 1) * d, :])

    mix_out = [mix(c, r) for c, r in enumerate(chains)]
    x1, hn, acc = [], [], []
    for c, r in enumerate(chains):
        x1.append(x_ref[r, :] + _rms(mix_out[c], gpm_ref[...]))
        hn.append(_rms(x1[c], gpre_ref[...]).astype(BF16))
        acc.append(mlp_chunk(hn[c], 0))
    for j in range(1, ff // d):
        for c in range(OUT_CHAINS):
            acc[c] = acc[c] + mlp_chunk(hn[c], j)
    for c, r in enumerate(chains):
        o_ref[r, :] = x1[c] + _rms(acc[c], gpost_ref[...])


def _output_block(x2, ym, ys, gate, wum, wus, wo, gpm, gpre, wmi, wmo, gpost):
    n, d = x2.shape
    tm = OUT_TILE
    tok = lambda i: (i, 0)
    const = lambda i: (0, 0)
    resident = lambda a: pl.BlockSpec(a.shape, const, pipeline_mode=pl.Buffered(1))
    return pl.pallas_call(
        _out_kernel,
        grid=(n // tm,),
        in_specs=[
            pl.BlockSpec((tm, d), tok),
            pl.BlockSpec((tm // KEY_BLOCK,) + ym.shape[1:], lambda i: (i, 0, 0)),
            pl.BlockSpec((tm, ys.shape[1]), tok),
            pl.BlockSpec((tm, gate.shape[1]), tok),
            resident(wum), resident(wus), resident(wo), resident(gpm),
            resident(gpre), resident(wmi), resident(wmo), resident(gpost),
        ],
        out_specs=pl.BlockSpec((tm, d), tok),
        out_shape=jax.ShapeDtypeStruct((n, d), F32),
        compiler_params=pltpu.CompilerParams(
            dimension_semantics=("arbitrary",), vmem_limit_bytes=VMEM_LIMIT),
    )(x2, ym, ys, gate, wum, wus, wo, gpm, gpre, wmi, wmo, gpost)


def _rope_tables(seq):
    half = HEAD_DIM // 2
    inv_freq = ROPE_THETA ** (-jnp.arange(half, dtype=F32) * 2.0 / HEAD_DIM)
    ang = jnp.arange(seq, dtype=F32)[:, None] * inv_freq[None, :]
    ang = jnp.concatenate([ang] * (2 * HEADS_PER_TILE), axis=-1)
    cos, sin = jnp.cos(ang), jnp.sin(ang)
    first_half = (jnp.arange(LANES) % HEAD_DIM) < half
    return cos, jnp.where(first_half, -sin, 0.0), jnp.where(first_half, 0.0, sin)


def kernel(x, g_pre_mix, w_in, b_gate, w_up_moba, w_up_sb, w_out, g_post_mix,
           g_pre_mlp, w_mlp_in, w_mlp_out, g_post_mlp):
    b, s, d = x.shape
    depth = w_in.shape[0]
    cos, sina, sinb = _rope_tables(s)
    for l in range(depth):
        w = BRANCH_WIDTH
        qa, kat, vat, qb, kbt, vb, gate, kmean = _in_projection(
            x, g_pre_mix[l][None], w_in[l].astype(BF16), b_gate[l][None], cos, sina, sinb)
        nkb = kmean.shape[1]
        y_moba = _attention_call(
            _moba_kernel, qa, kat, vat, True, [kmean.reshape(b, nkb, -1)],
            [pl.BlockSpec((1, nkb, LANES), lambda bi, hp: (bi, 0, hp))],
            [pltpu.VMEM((HEADS_PER_TILE, LANES, s), BF16),
             pltpu.VMEM((HEADS_PER_TILE, HEAD_DIM + SUBLANES, s), BF16)])
        y_sb = _attention_call(
            _sb_kernel, qb, kbt, vb, False, [], [],
            [pltpu.VMEM((2 * SB_TILE_GROUP, KEY_BLOCK, LANES), F32)] * 2)
        x = _output_block(
            x.reshape(b * s, d), y_moba.reshape(b * nkb, w, -1), y_sb.reshape(b * s, -1),
            gate.reshape(b * s, -1),
            w_up_moba[l].astype(BF16), w_up_sb[l].astype(BF16), w_out[l].astype(BF16),
            g_post_mix[l][None], g_pre_mlp[l][None],
            w_mlp_in[l].astype(BF16), w_mlp_out[l].astype(BF16), g_post_mlp[l][None],
        ).reshape(b, s, d)
    return x
```

```python
import math

import jax
import jax.numpy as jnp
from jax import lax
from jax.experimental import pallas as pl
from jax.experimental.pallas import tpu as pltpu

F32 = jnp.float32
BF16 = jnp.bfloat16

HEAD_DIM = 64
N_HEADS = 8
BRANCH_WIDTH = N_HEADS * HEAD_DIM
KEY_BLOCK = 256
TOPK = 3
ROPE_THETA = 10000.0
RMS_EPS = 1e-6
NEG = -1e30
LANES = 128
SUBLANES = 8
HEADS_PER_TILE = LANES // HEAD_DIM
VMEM_LIMIT = 56 * 1024 * 1024

GATE_GROUP = 8
GATE_GROUP_SHIFT = 3
SB_DEAD_LOG2 = -126.0
LOG2_E = 1.4426950408889634
SB_TILE_GROUP = 8
FIRST_GATED_TILE = TOPK + 1
N_GATED_TILES = 4
MOBA_TILE_ORDER = (3, 2, 1, 0, 7, 6, 5, 4)

FEATURE_MAJOR_COLUMNS = (2, 4)

IN_TILE = 1024
OUT_TILE = 512
OUT_CHAINS = 2


def _dot(a, b):
    return jnp.dot(a, b, preferred_element_type=F32)


def _dot_nt(a, b):
    return lax.dot_general(a, b, (((1,), (1,)), ((), ())), preferred_element_type=F32)


def _rms(x, g):
    ms = jnp.mean(x * x, axis=-1, keepdims=True)
    return x * lax.rsqrt(ms + RMS_EPS) * g


def _inproj_kernel(x_ref, g_ref, w_ref, b_ref, cos_ref, sina_ref, sinb_ref,
                   qa_ref, kat_ref, vat_ref, qb_ref, kbt_ref, vb_ref, gate_ref, km_ref, wt_ref):
    w = BRANCH_WIDTH

    @pl.when((pl.program_id(0) == 0) & (pl.program_id(1) == 0))
    def _():
        for slot, c in enumerate(FEATURE_MAJOR_COLUMNS):
            wt_ref[slot] = w_ref[:, c * w:(c + 1) * w].astype(F32).T.astype(BF16)

    reps = w // LANES
    scale = LOG2_E / math.sqrt(HEAD_DIM)
    half = HEAD_DIM // 2
    blocks = [pl.ds(i * KEY_BLOCK, KEY_BLOCK) for i in range(x_ref.shape[1] // KEY_BLOCK)]
    hns = [_rms(x_ref[0, r, :], g_ref[...]).astype(BF16) for r in blocks]

    tables = [[jnp.concatenate([ref[r, :]] * reps, axis=1) for ref in (cos_ref, sina_ref, sinb_ref)]
              for r in blocks]

    def proj(hn, c):
        return _dot(hn, w_ref[:, c * w:(c + 1) * w])

    def rope(t, i):
        cos, sina, sinb = tables[i]
        return t * cos + pltpu.roll(t, w - half, 1) * sina + pltpu.roll(t, half, 1) * sinb

    chains = list(enumerate(zip(blocks, hns)))
    for i, (r, hn) in chains:
        qa_ref[0, r, :] = (rope(proj(hn, 0), i) * scale).astype(BF16)
    for i, (r, hn) in chains:
        ka = rope(proj(hn, 1), i)
        km_ref[0, i] = jnp.sum(ka, axis=0, keepdims=True) * (1.0 / KEY_BLOCK)
        kat_ref[0, i] = ka.T.astype(BF16)
    for i, (r, hn) in chains:
        vat_ref[0, i] = _dot_nt(wt_ref[0], hn).astype(BF16)
    for i, (r, hn) in chains:
        qb_ref[0, r, :] = (proj(hn, 3) * scale).astype(BF16)
    for i, (r, hn) in chains:
        kbt_ref[0, i] = _dot_nt(wt_ref[1], hn).astype(BF16)
    for i, (r, hn) in chains:
        vb_ref[0, r, :] = proj(hn, 5).astype(BF16)
    for i, (r, hn) in chains:
        gates = _dot(hn, w_ref[:, 6 * w:]) + b_ref[...]
        gate_ref[0, r, :] = jax.nn.sigmoid(gates).astype(BF16)


def _in_projection(x, g_pre, w_in, b_gate, cos, sina, sinb):
    b, s, d = x.shape
    tm = IN_TILE
    w = BRANCH_WIDTH
    gw = w_in.shape[1] - 6 * w
    nkb = s // KEY_BLOCK
    const = lambda i, j: (0, 0)
    tok = lambda i, j: (j, i, 0)
    tab = lambda i, j: (i, 0)
    branch = jax.ShapeDtypeStruct((b, s, w), BF16)
    branch_spec = pl.BlockSpec((1, tm, w), tok)
    feat_major = jax.ShapeDtypeStruct((b, nkb, w, KEY_BLOCK), BF16)
    feat_major_spec = pl.BlockSpec((1, tm // KEY_BLOCK, w, KEY_BLOCK), lambda i, j: (j, i, 0, 0))
    resident = lambda a: pl.BlockSpec(a.shape, const, pipeline_mode=pl.Buffered(1))
    return pl.pallas_call(
        _inproj_kernel,
        grid=(s // tm, b),
        in_specs=[
            pl.BlockSpec((1, tm, d), tok),
            pl.BlockSpec((1, d), const),
            resident(w_in),
            pl.BlockSpec((1, gw), const),
            pl.BlockSpec((tm, LANES), tab),
            pl.BlockSpec((tm, LANES), tab),
            pl.BlockSpec((tm, LANES), tab),
        ],
        out_specs=[branch_spec, feat_major_spec, feat_major_spec, branch_spec, feat_major_spec, branch_spec] + [
            pl.BlockSpec((1, tm, gw), tok),
            pl.BlockSpec((1, tm // KEY_BLOCK, 1, w), lambda i, j: (j, i, 0, 0)),
        ],
        out_shape=[branch, feat_major, feat_major, branch, feat_major, branch] + [
            jax.ShapeDtypeStruct((b, s, gw), BF16),
            jax.ShapeDtypeStruct((b, nkb, 1, w), F32),
        ],
        scratch_shapes=[pltpu.VMEM((len(FEATURE_MAJOR_COLUMNS), w, d), BF16)],
        compiler_params=pltpu.CompilerParams(
            dimension_semantics=("arbitrary", "arbitrary"), vmem_limit_bytes=VMEM_LIMIT),
    )(x, g_pre, w_in, b_gate, cos, sina, sinb)


def _head_masks(shape):
    lane = lax.broadcasted_iota(jnp.int32, shape, len(shape) - 1)
    return [(lane >= h * HEAD_DIM) & (lane < (h + 1) * HEAD_DIM) for h in range(HEADS_PER_TILE)]


def _merge_heads(outs, masks):
    y = outs[0]
    for o, m in zip(outs[1:], masks[1:]):
        y = jnp.where(m, o, y)
    return y


def _gate_lane_offset(h):
    return HEAD_DIM * (HEADS_PER_TILE - 1 - h)


def _gate_lane(row_half, h, gated_tile):
    return GATE_GROUP * ((row_half * HEADS_PER_TILE + h) * N_GATED_TILES + gated_tile)


def _moba_bias(q_ref, km_ref, t, n_tiles):
    half = t // 2
    assert n_tiles == FIRST_GATED_TILE + N_GATED_TILES and 2 * HEADS_PER_TILE * N_GATED_TILES * GATE_GROUP == LANES
    lane = lax.broadcasted_iota(jnp.int32, (half, LANES), 1)
    masks = _head_masks((n_tiles, LANES))
    km = km_ref[0]
    km_hi, km_lo = [], []
    for h in range(HEADS_PER_TILE):
        kmh = jnp.where(masks[h], km, 0.0)
        km_hi.append(kmh.astype(BF16))
        km_lo.append((kmh - km_hi[h].astype(F32)).astype(BF16))

    def out_rows(vals, row_half, g):
        parts, at = [], 0
        for h in range(HEADS_PER_TILE):
            start = _gate_lane(row_half, h, g)
            if start > at:
                parts.append(jnp.zeros((start - at, LANES), BF16))
            parts.append(vals[h])
            at = start + n_tiles
        if at < LANES:
            parts.append(jnp.zeros((LANES - at, LANES), BF16))
        return jnp.concatenate(parts, axis=0)

    pieces = [(row_half, g) for row_half in range(2) for g in range(N_GATED_TILES)]
    w_hi = jnp.concatenate([out_rows(km_hi, s, g) for s, g in pieces], axis=1)
    w_lo = jnp.concatenate([out_rows(km_lo, s, g) for s, g in pieces], axis=1)
    q_all = jnp.concatenate(
        [q_ref[0, (FIRST_GATED_TILE + g) * t + s * half:(FIRST_GATED_TILE + g) * t + (s + 1) * half, :]
         for s, g in pieces], axis=1)
    gate = _dot_nt(q_all, w_hi) + _dot_nt(q_all, w_lo)
    yield None

    blk = lane & (GATE_GROUP - 1)
    tile = ((lane >> GATE_GROUP_SHIFT) & (N_GATED_TILES - 1)) + FIRST_GATED_TILE
    beaten = jnp.zeros((half, LANES), F32)
    for d in range(1, GATE_GROUP):
        wraps = blk + d >= GATE_GROUP
        other = jnp.where(wraps, pltpu.roll(gate, GATE_GROUP - d, 1), pltpu.roll(gate, LANES - d, 1))
        other_blk = jnp.where(wraps, blk + d - GATE_GROUP, blk + d)
        wins = (other > gate) | ((other == gate) & wraps)
        beaten = beaten + jnp.where(wins & (other_blk < tile), 1.0, 0.0)
        yield None
    dropped = (blk < tile) & (beaten >= TOPK)
    yield jnp.where(dropped, NEG, 0.0)


def _moba_kernel(q_ref, kt_ref, vt_ref, km_ref, o_ref, kaug_ref, vaug_ref):
    t = KEY_BLOCK
    n_tiles = q_ref.shape[1] // t
    assert n_tiles == GATE_GROUP
    lane = lax.broadcasted_iota(jnp.int32, (t, LANES), 1)
    masks = _head_masks((t, LANES))
    row = lax.broadcasted_iota(jnp.int32, (t, t), 0)
    col = lax.broadcasted_iota(jnp.int32, (t, t), 1)
    causal = col <= row

    feat = lax.broadcasted_iota(jnp.int32, (LANES, t), 0)
    for h in range(HEADS_PER_TILE):
        off = _gate_lane_offset(h)
        in_head = (feat >= h * HEAD_DIM) & (feat < (h + 1) * HEAD_DIM)
        for n in range(n_tiles):
            onehot = jnp.where(feat == off + n, 1.0, 0.0).astype(BF16)
            kaug_ref[h, :, n * t:(n + 1) * t] = jnp.where(in_head, kt_ref[0, n], onehot)

    ones_row = jnp.where(lax.broadcasted_iota(jnp.int32, (SUBLANES, t), 0) == 0, 1.0, 0.0).astype(BF16)
    for h in range(HEADS_PER_TILE):
        for n in range(n_tiles):
            vaug_ref[h, :, n * t:(n + 1) * t] = jnp.concatenate(
                [vt_ref[0, n, h * HEAD_DIM:(h + 1) * HEAD_DIM, :], ones_row], axis=0)

    bias_steps = _moba_bias(q_ref, km_ref, t, n_tiles)
    bias_all = []

    def scores(qi, h):
        q2 = q_ref[0, qi * t:(qi + 1) * t, :]
        qh = jnp.where(masks[h], q2, jnp.zeros_like(q2))
        if qi >= FIRST_GATED_TILE:
            bias_all.extend(step for step in bias_steps if step is not None)
            off = _gate_lane_offset(h)
            halves = []
            for row_half in range(2):
                shift = (off - _gate_lane(row_half, h, qi - FIRST_GATED_TILE)) % LANES
                halves.append(pltpu.roll(bias_all[0], shift, 1) if shift else bias_all[0])
            bias_qi = jnp.concatenate(halves, axis=0)
            in_group = (lane >= off) & (lane < off + GATE_GROUP)
            qh = (qh.astype(F32) + jnp.where(in_group, bias_qi, 0.0)).astype(BF16)
        return _dot(qh, kaug_ref[h, :, 0:(qi + 1) * t])

    def attend(qi, h, s):
        nk = (qi + 1) * t
        s_own = jnp.where(causal, s[:, qi * t:], NEG)
        m = jnp.max(s_own, axis=-1, keepdims=True)
        if qi:
            s_past = s[:, :qi * t]
            m = jnp.maximum(m, jnp.max(s_past, axis=-1, keepdims=True))
            p = jnp.concatenate([jnp.exp2((s_past - m).astype(BF16)),
                                 jnp.exp2((s_own - m).astype(BF16))], axis=1)
        else:
            p = jnp.exp2((s_own - m).astype(BF16))
        num_den = _dot_nt(vaug_ref[h, :, 0:nk], p)
        return (num_den[:HEAD_DIM] / num_den[HEAD_DIM:HEAD_DIM + 1]).astype(o_ref.dtype)

    chains = [(qi, h) for qi in MOBA_TILE_ORDER for h in range(HEADS_PER_TILE)]
    s_next = scores(*chains[0])
    for i, (qi, h) in enumerate(chains):
        s = s_next
        if i + 1 < len(chains):
            s_next = scores(*chains[i + 1])
        o_ref[0, qi, h * HEAD_DIM:(h + 1) * HEAD_DIM, :] = attend(qi, h, s)
        if not bias_all:
            step = next(bias_steps)
            if step is not None:
                bias_all.append(step)


def _attention_call(kernel, q, kt, v, v_feature_major, extra_inputs, extra_specs, scratch_shapes):
    b, s, w = q.shape
    spec = pl.BlockSpec((1, s, LANES), lambda bi, hp: (bi, 0, hp))
    kt_spec = pl.BlockSpec((1, s // KEY_BLOCK, LANES, KEY_BLOCK), lambda bi, hp: (bi, 0, hp, 0))
    v_spec = kt_spec if v_feature_major else spec
    return pl.pallas_call(
        kernel,
        grid=(b, w // LANES),
        in_specs=[spec, kt_spec, v_spec] + extra_specs,
        out_specs=v_spec,
        out_shape=jax.ShapeDtypeStruct(v.shape, BF16),
        scratch_shapes=scratch_shapes,
        compiler_params=pltpu.CompilerParams(
            dimension_semantics=("arbitrary", "arbitrary"), vmem_limit_bytes=VMEM_LIMIT),
    )(q, kt, v, *extra_inputs)


def _sb_kernel(q_ref, kt_ref, v_ref, o_ref, acc_ref, carry_ref):
    t = KEY_BLOCK
    half = t // 2
    rows = HEADS_PER_TILE * half
    n_tiles = q_ref.shape[1] // t
    masks = _head_masks((half, LANES))
    tri = jnp.where(lax.broadcasted_iota(jnp.int32, (t, t), 0) > lax.broadcasted_iota(jnp.int32, (t, t), 1),
                    1.0, 0.0).astype(BF16)

    def earlier_keys(nk, first_query):
        q_pos = (lax.broadcasted_iota(jnp.int32, (rows, nk), 0) & (half - 1)) + first_query
        return lax.broadcasted_iota(jnp.int32, (rows, nk), 1) < q_pos

    def logits(q_start, block, nk):
        q2 = q_ref[0, pl.ds(q_start, half), :]
        qs = jnp.concatenate([jnp.where(m, q2, jnp.zeros_like(q2)) for m in masks], axis=0)
        return _dot(qs, kt_ref[0, block, :, 0:nk])

    def decay(z, nk, earlier):
        if earlier is not None:
            z = jnp.where(earlier, z, NEG)
        log_beta = jnp.minimum(z, 0.0) - jnp.log2(1.0 + jnp.exp2(-jnp.abs(z)))
        log_1mb = log_beta - z
        suffix = _dot(log_1mb.astype(BF16), tri[:nk, :nk])
        total = suffix[:, 0:1] + log_1mb[:, 0:1]
        return log_beta, suffix, jnp.broadcast_to(total, (rows, LANES))

    def weighted_values(log_beta, suffix, block, nk):
        v2 = v_ref[0, pl.ds(pl.multiple_of(block * t, t), nk), :]
        return _dot(jnp.exp2(log_beta + suffix).astype(BF16), v2)

    def weigh(z, block, nk, earlier):
        log_beta, suffix, total = decay(z, nk, earlier)
        return weighted_values(log_beta, suffix, block, nk), total

    def alive(slot):
        return (jnp.max(carry_ref[slot]) > SB_DEAD_LOG2).astype(jnp.int32)

    def finish(slot, q_start, first_block, live):
        def more(state):
            n, live = state
            return (n >= 0) & (live > 0)

        def earlier_block(state):
            n, _ = state
            pv, total = weigh(logits(q_start, n, t), n, t, None)
            carry = carry_ref[slot]
            acc_ref[slot] = acc_ref[slot] + jnp.exp2(carry) * pv
            carry_ref[slot] = carry + total
            return n - 1, alive(slot)

        if first_block >= 0:
            lax.while_loop(more, earlier_block, (jnp.int32(first_block), live))
        acc = acc_ref[slot]
        o_ref[0, q_start:q_start + half, :] = _merge_heads(
            [acc[h * half:(h + 1) * half] for h in range(HEADS_PER_TILE)], masks).astype(o_ref.dtype)

    def tile_group(tiles):
        slabs = []
        pending = []
        for i, qi in enumerate(tiles):
            top = qi * t
            slabs.append((2 * i, top, qi, half, earlier_keys(half, 0)))
            if qi:
                slabs.append((2 * i, top, qi - 1, t, None))
            slabs.append((2 * i + 1, top + half, qi, t, earlier_keys(t, half)))
            pending += [(2 * i, top, qi - 2), (2 * i + 1, top + half, qi - 1)]
        n = len(slabs)
        z = {j: logits(*slabs[j][1:4]) for j in range(min(2, n))}
        mid = {0: decay(z.pop(0), slabs[0][3], slabs[0][4])}
        started = set()
        for j, (slot, q_start, block, nk, mask) in enumerate(slabs):
            if j + 2 < n:
                z[j + 2] = logits(*slabs[j + 2][1:4])
            if j + 1 < n:
                mid[j + 1] = decay(z.pop(j + 1), slabs[j + 1][3], slabs[j + 1][4])
            log_beta, suffix, total = mid.pop(j)
            pv = weighted_values(log_beta, suffix, block, nk)
            if slot in started:
                carry = carry_ref[slot]
                acc_ref[slot] = acc_ref[slot] + jnp.exp2(carry) * pv
                carry_ref[slot] = carry + total
            else:
                acc_ref[slot] = pv
                carry_ref[slot] = total
                started.add(slot)
        lives = [alive(slot) if first_block >= 0 else None for slot, _, first_block in pending]
        for (slot, q_start, first_block), live in zip(pending, lives):
            finish(slot, q_start, first_block, live)

    for first in range(0, n_tiles, SB_TILE_GROUP):
        tile_group(range(first, first + SB_TILE_GROUP))


def _out_kernel(x_ref, ym_ref, ys_ref, gate_ref, wum_ref, wus_ref, wo_ref, gpm_ref,
                gpre_ref, wmi_ref, wmo_ref, gpost_ref, o_ref):
    d = x_ref.shape[-1]
    ff = wmi_ref.shape[1]
    rows = x_ref.shape[0] // OUT_CHAINS
    assert rows == KEY_BLOCK
    chains = [pl.ds(c * rows, rows) for c in range(OUT_CHAINS)]

    def mix(c, r):
        up_m = lax.dot_general(ym_ref[c], wum_ref[...], (((0,), (0,)), ((), ())), preferred_element_type=F32)
        up_s = _dot(ys_ref[r, :], wus_ref[...])
        mixed = gate_ref[r, :d].astype(F32) * up_m + gate_ref[r, d:].astype(F32) * up_s
        return _dot(mixed.astype(BF16), wo_ref[...])

    def mlp_chunk(hn, j):
        u = _dot(hn, wmi_ref[:, j * d:(j + 1) * d])
        u = jnp.square(jnp.maximum(u, 0.0))
        return _dot(u.astype(BF16), wmo_ref[j * d:(j + 1) * d, :])

    mix_out = [mix(c, r) for c, r in enumerate(chains)]
    x1, hn, acc = [], [], []
    for c, r in enumerate(chains):
        x1.append(x_ref[r, :] + _rms(mix_out[c], gpm_ref[...]))
        hn.append(_rms(x1[c], gpre_ref[...]).astype(BF16))
        acc.append(mlp_chunk(hn[c], 0))
    for j in range(1, ff // d):
        for c in range(OUT_CHAINS):
            acc[c] = acc[c] + mlp_chunk(hn[c], j)
    for c, r in enumerate(chains):
        o_ref[r, :] = x1[c] + _rms(acc[c], gpost_ref[...])


def _output_block(x2, ym, ys, gate, wum, wus, wo, gpm, gpre, wmi, wmo, gpost):
    n, d = x2.shape
    tm = OUT_TILE
    tok = lambda i: (i, 0)
    const = lambda i: (0, 0)
    resident = lambda a: pl.BlockSpec(a.shape, const, pipeline_mode=pl.Buffered(1))
    return pl.pallas_call(
        _out_kernel,
        grid=(n // tm,),
        in_specs=[
            pl.BlockSpec((tm, d), tok),
            pl.BlockSpec((tm // KEY_BLOCK,) + ym.shape[1:], lambda i: (i, 0, 0)),
            pl.BlockSpec((tm, ys.shape[1]), tok),
            pl.BlockSpec((tm, gate.shape[1]), tok),
            resident(wum), resident(wus), resident(wo), resident(gpm),
            resident(gpre), resident(wmi), resident(wmo), resident(gpost),
        ],
        out_specs=pl.BlockSpec((tm, d), tok),
        out_shape=jax.ShapeDtypeStruct((n, d), F32),
        compiler_params=pltpu.CompilerParams(
            dimension_semantics=("arbitrary",), vmem_limit_bytes=VMEM_LIMIT),
    )(x2, ym, ys, gate, wum, wus, wo, gpm, gpre, wmi, wmo, gpost)


def _rope_tables(seq):
    half = HEAD_DIM // 2
    inv_freq = ROPE_THETA ** (-jnp.arange(half, dtype=F32) * 2.0 / HEAD_DIM)
    ang = jnp.arange(seq, dtype=F32)[:, None] * inv_freq[None, :]
    ang = jnp.concatenate([ang] * (2 * HEADS_PER_TILE), axis=-1)
    cos, sin = jnp.cos(ang), jnp.sin(ang)
    first_half = (jnp.arange(LANES) % HEAD_DIM) < half
    return cos, jnp.where(first_half, -sin, 0.0), jnp.where(first_half, 0.0, sin)


def kernel(x, g_pre_mix, w_in, b_gate, w_up_moba, w_up_sb, w_out, g_post_mix,
           g_pre_mlp, w_mlp_in, w_mlp_out, g_post_mlp):
    b, s, d = x.shape
    depth = w_in.shape[0]
    cos, sina, sinb = _rope_tables(s)
    for l in range(depth):
        w = BRANCH_WIDTH
        qa, kat, vat, qb, kbt, vb, gate, kmean = _in_projection(
            x, g_pre_mix[l][None], w_in[l].astype(BF16), b_gate[l][None], cos, sina, sinb)
        nkb = kmean.shape[1]
        y_moba = _attention_call(
            _moba_kernel, qa, kat, vat, True, [kmean.reshape(b, nkb, -1)],
            [pl.BlockSpec((1, nkb, LANES), lambda bi, hp: (bi, 0, hp))],
            [pltpu.VMEM((HEADS_PER_TILE, LANES, s), BF16),
             pltpu.VMEM((HEADS_PER_TILE, HEAD_DIM + SUBLANES, s), BF16)])
        y_sb = _attention_call(
            _sb_kernel, qb, kbt, vb, False, [], [],
            [pltpu.VMEM((2 * SB_TILE_GROUP, KEY_BLOCK, LANES), F32)] * 2)
        x = _output_block(
            x.reshape(b * s, d), y_moba.reshape(b * nkb, w, -1), y_sb.reshape(b * s, -1),
            gate.reshape(b * s, -1),
            w_up_moba[l].astype(BF16), w_up_sb[l].astype(BF16), w_out[l].astype(BF16),
            g_post_mix[l][None], g_pre_mlp[l][None],
            w_mlp_in[l].astype(BF16), w_mlp_out[l].astype(BF16), g_post_mlp[l][None],
        ).reshape(b, s, d)
    return x
```

```python
import math

import jax
import jax.numpy as jnp
from jax import lax
from jax.experimental import pallas as pl
from jax.experimental.pallas import tpu as pltpu

F32 = jnp.float32
BF16 = jnp.bfloat16

HEAD_DIM = 64
N_HEADS = 8
BRANCH_WIDTH = N_HEADS * HEAD_DIM
KEY_BLOCK = 256
TOPK = 3
ROPE_THETA = 10000.0
RMS_EPS = 1e-6
NEG = -1e30
LANES = 128
SUBLANES = 8
HEADS_PER_TILE = LANES // HEAD_DIM
VMEM_LIMIT = 56 * 1024 * 1024

GATE_GROUP = 8
GATE_GROUP_SHIFT = 3
SB_DEAD_LOG2 = -126.0
LOG2_E = 1.4426950408889634
SB_TILE_GROUP = 8
FIRST_GATED_TILE = TOPK + 1
N_GATED_TILES = 4
MOBA_TILE_ORDER = (3, 2, 1, 0, 7, 6, 5, 4)
MOBA_WORK = 72.0
SB_WORK = 23.0

FEATURE_MAJOR_COLUMNS = (2, 4)

IN_TILE = 1024
OUT_TILE = 512
OUT_CHAINS = 2


def _dot(a, b):
    return jnp.dot(a, b, preferred_element_type=F32)


def _dot_nt(a, b):
    return lax.dot_general(a, b, (((1,), (1,)), ((), ())), preferred_element_type=F32)


def _rms(x, g):
    ms = jnp.mean(x * x, axis=-1, keepdims=True)
    return x * lax.rsqrt(ms + RMS_EPS) * g


def _inproj_kernel(x_ref, g_ref, w_ref, b_ref, cos_ref, sina_ref, sinb_ref,
                   qa_ref, kat_ref, vat_ref, qb_ref, kbt_ref, vb_ref, gate_ref, km_ref, wt_ref):
    w = BRANCH_WIDTH

    @pl.when((pl.program_id(0) == 0) & (pl.program_id(1) == 0))
    def _():
        for slot, c in enumerate(FEATURE_MAJOR_COLUMNS):
            wt_ref[slot] = w_ref[:, c * w:(c + 1) * w].astype(F32).T.astype(BF16)

    reps = w // LANES
    scale = LOG2_E / math.sqrt(HEAD_DIM)
    half = HEAD_DIM // 2
    blocks = [pl.ds(i * KEY_BLOCK, KEY_BLOCK) for i in range(x_ref.shape[1] // KEY_BLOCK)]
    hns = [_rms(x_ref[0, r, :], g_ref[...]).astype(BF16) for r in blocks]

    tables = [[jnp.concatenate([ref[r, :]] * reps, axis=1) for ref in (cos_ref, sina_ref, sinb_ref)]
              for r in blocks]

    def proj(hn, c):
        return _dot(hn, w_ref[:, c * w:(c + 1) * w])

    def rope(t, i):
        cos, sina, sinb = tables[i]
        return t * cos + pltpu.roll(t, w - half, 1) * sina + pltpu.roll(t, half, 1) * sinb

    chains = list(enumerate(zip(blocks, hns)))
    for i, (r, hn) in chains:
        qa_ref[0, r, :] = (rope(proj(hn, 0), i) * scale).astype(BF16)
    for i, (r, hn) in chains:
        ka = rope(proj(hn, 1), i)
        km_ref[0, i] = jnp.sum(ka, axis=0, keepdims=True) * (1.0 / KEY_BLOCK)
        kat_ref[0, i] = ka.T.astype(BF16)
    for i, (r, hn) in chains:
        vat_ref[0, i] = _dot_nt(wt_ref[0], hn).astype(BF16)
    for i, (r, hn) in chains:
        qb_ref[0, r, :] = (proj(hn, 3) * scale).astype(BF16)
    for i, (r, hn) in chains:
        kbt_ref[0, i] = _dot_nt(wt_ref[1], hn).astype(BF16)
    for i, (r, hn) in chains:
        vb_ref[0, r, :] = proj(hn, 5).astype(BF16)
    for i, (r, hn) in chains:
        gates = _dot(hn, w_ref[:, 6 * w:]) + b_ref[...]
        gate_ref[0, r, :] = jax.nn.sigmoid(gates).astype(BF16)


def _in_projection(x, g_pre, w_in, b_gate, cos, sina, sinb):
    b, s, d = x.shape
    tm = IN_TILE
    w = BRANCH_WIDTH
    gw = w_in.shape[1] - 6 * w
    nkb = s // KEY_BLOCK
    const = lambda i, j: (0, 0)
    tok = lambda i, j: (j, i, 0)
    tab = lambda i, j: (i, 0)
    branch = jax.ShapeDtypeStruct((b, s, w), BF16)
    branch_spec = pl.BlockSpec((1, tm, w), tok)
    feat_major = jax.ShapeDtypeStruct((b, nkb, w, KEY_BLOCK), BF16)
    feat_major_spec = pl.BlockSpec((1, tm // KEY_BLOCK, w, KEY_BLOCK), lambda i, j: (j, i, 0, 0))
    resident = lambda a: pl.BlockSpec(a.shape, const, pipeline_mode=pl.Buffered(1))
    return pl.pallas_call(
        _inproj_kernel,
        grid=(s // tm, b),
        in_specs=[
            pl.BlockSpec((1, tm, d), tok),
            pl.BlockSpec((1, d), const),
            resident(w_in),
            pl.BlockSpec((1, gw), const),
            pl.BlockSpec((tm, LANES), tab),
            pl.BlockSpec((tm, LANES), tab),
            pl.BlockSpec((tm, LANES), tab),
        ],
        out_specs=[branch_spec, feat_major_spec, feat_major_spec, branch_spec, feat_major_spec, branch_spec] + [
            pl.BlockSpec((1, tm, gw), tok),
            pl.BlockSpec((1, tm // KEY_BLOCK, 1, w), lambda i, j: (j, i, 0, 0)),
        ],
        out_shape=[branch, feat_major, feat_major, branch, feat_major, branch] + [
            jax.ShapeDtypeStruct((b, s, gw), BF16),
            jax.ShapeDtypeStruct((b, nkb, 1, w), F32),
        ],
        scratch_shapes=[pltpu.VMEM((len(FEATURE_MAJOR_COLUMNS), w, d), BF16)],
        compiler_params=pltpu.CompilerParams(
            dimension_semantics=("arbitrary", "arbitrary"), vmem_limit_bytes=VMEM_LIMIT),
    )(x, g_pre, w_in, b_gate, cos, sina, sinb)


def _head_masks(shape):
    lane = lax.broadcasted_iota(jnp.int32, shape, len(shape) - 1)
    return [(lane >= h * HEAD_DIM) & (lane < (h + 1) * HEAD_DIM) for h in range(HEADS_PER_TILE)]


def _merge_heads(outs, masks):
    y = outs[0]
    for o, m in zip(outs[1:], masks[1:]):
        y = jnp.where(m, o, y)
    return y


def _gate_lane_offset(h):
    return HEAD_DIM * (HEADS_PER_TILE - 1 - h)


def _gate_lane(row_half, h, gated_tile):
    return GATE_GROUP * ((row_half * HEADS_PER_TILE + h) * N_GATED_TILES + gated_tile)


def _moba_bias(q_ref, km_ref, t, n_tiles):
    half = t // 2
    assert n_tiles == FIRST_GATED_TILE + N_GATED_TILES and 2 * HEADS_PER_TILE * N_GATED_TILES * GATE_GROUP == LANES
    lane = lax.broadcasted_iota(jnp.int32, (half, LANES), 1)
    masks = _head_masks((n_tiles, LANES))
    km = km_ref[0]
    km_hi, km_lo = [], []
    for h in range(HEADS_PER_TILE):
        kmh = jnp.where(masks[h], km, 0.0)
        km_hi.append(kmh.astype(BF16))
        km_lo.append((kmh - km_hi[h].astype(F32)).astype(BF16))

    def out_rows(vals, row_half, g):
        parts, at = [], 0
        for h in range(HEADS_PER_TILE):
            start = _gate_lane(row_half, h, g)
            if start > at:
                parts.append(jnp.zeros((start - at, LANES), BF16))
            parts.append(vals[h])
            at = start + n_tiles
        if at < LANES:
            parts.append(jnp.zeros((LANES - at, LANES), BF16))
        return jnp.concatenate(parts, axis=0)

    pieces = [(row_half, g) for row_half in range(2) for g in range(N_GATED_TILES)]
    w_hi = jnp.concatenate([out_rows(km_hi, s, g) for s, g in pieces], axis=1)
    w_lo = jnp.concatenate([out_rows(km_lo, s, g) for s, g in pieces], axis=1)
    q_all = jnp.concatenate(
        [q_ref[0, (FIRST_GATED_TILE + g) * t + s * half:(FIRST_GATED_TILE + g) * t + (s + 1) * half, :]
         for s, g in pieces], axis=1)
    gate = _dot_nt(q_all, w_hi) + _dot_nt(q_all, w_lo)
    yield None

    blk = lane & (GATE_GROUP - 1)
    tile = ((lane >> GATE_GROUP_SHIFT) & (N_GATED_TILES - 1)) + FIRST_GATED_TILE
    beaten = jnp.zeros((half, LANES), F32)
    for d in range(1, GATE_GROUP):
        wraps = blk + d >= GATE_GROUP
        other = jnp.where(wraps, pltpu.roll(gate, GATE_GROUP - d, 1), pltpu.roll(gate, LANES - d, 1))
        other_blk = jnp.where(wraps, blk + d - GATE_GROUP, blk + d)
        wins = (other > gate) | ((other == gate) & wraps)
        beaten = beaten + jnp.where(wins & (other_blk < tile), 1.0, 0.0)
        yield None
    dropped = (blk < tile) & (beaten >= TOPK)
    yield jnp.where(dropped, NEG, 0.0)


def _moba_chains(q_ref, kt_ref, vt_ref, km_ref, o_ref, kaug_ref, vaug_ref):
    t = KEY_BLOCK
    n_tiles = q_ref.shape[1] // t
    assert n_tiles == GATE_GROUP
    lane = lax.broadcasted_iota(jnp.int32, (t, LANES), 1)
    masks = _head_masks((t, LANES))
    row = lax.broadcasted_iota(jnp.int32, (t, t), 0)
    col = lax.broadcasted_iota(jnp.int32, (t, t), 1)
    causal = col <= row

    feat = lax.broadcasted_iota(jnp.int32, (LANES, t), 0)
    for h in range(HEADS_PER_TILE):
        off = _gate_lane_offset(h)
        in_head = (feat >= h * HEAD_DIM) & (feat < (h + 1) * HEAD_DIM)
        for n in range(n_tiles):
            onehot = jnp.where(feat == off + n, 1.0, 0.0).astype(BF16)
            kaug_ref[h, :, n * t:(n + 1) * t] = jnp.where(in_head, kt_ref[0, n], onehot)

    ones_row = jnp.where(lax.broadcasted_iota(jnp.int32, (SUBLANES, t), 0) == 0, 1.0, 0.0).astype(BF16)
    for h in range(HEADS_PER_TILE):
        for n in range(n_tiles):
            vaug_ref[h, :, n * t:(n + 1) * t] = jnp.concatenate(
                [vt_ref[0, n, h * HEAD_DIM:(h + 1) * HEAD_DIM, :], ones_row], axis=0)

    bias_steps = _moba_bias(q_ref, km_ref, t, n_tiles)
    bias_all = []

    def scores(qi, h):
        q2 = q_ref[0, qi * t:(qi + 1) * t, :]
        qh = jnp.where(masks[h], q2, jnp.zeros_like(q2))
        if qi >= FIRST_GATED_TILE:
            bias_all.extend(step for step in bias_steps if step is not None)
            off = _gate_lane_offset(h)
            halves = []
            for row_half in range(2):
                shift = (off - _gate_lane(row_half, h, qi - FIRST_GATED_TILE)) % LANES
                halves.append(pltpu.roll(bias_all[0], shift, 1) if shift else bias_all[0])
            bias_qi = jnp.concatenate(halves, axis=0)
            in_group = (lane >= off) & (lane < off + GATE_GROUP)
            qh = (qh.astype(F32) + jnp.where(in_group, bias_qi, 0.0)).astype(BF16)
        return _dot(qh, kaug_ref[h, :, 0:(qi + 1) * t])

    def attend(qi, h, s):
        nk = (qi + 1) * t
        s_own = jnp.where(causal, s[:, qi * t:], NEG)
        m = jnp.max(s_own, axis=-1, keepdims=True)
        if qi:
            s_past = s[:, :qi * t]
            m = jnp.maximum(m, jnp.max(s_past, axis=-1, keepdims=True))
            p = jnp.concatenate([jnp.exp2((s_past - m).astype(BF16)),
                                 jnp.exp2((s_own - m).astype(BF16))], axis=1)
        else:
            p = jnp.exp2((s_own - m).astype(BF16))
        num_den = _dot_nt(vaug_ref[h, :, 0:nk], p)
        return (num_den[:HEAD_DIM] / num_den[HEAD_DIM:HEAD_DIM + 1]).astype(o_ref.dtype)

    chains = [(qi, h) for qi in MOBA_TILE_ORDER for h in range(HEADS_PER_TILE)]
    s_next = scores(*chains[0])
    for i, (qi, h) in enumerate(chains):
        s = s_next
        if i + 1 < len(chains):
            s_next = scores(*chains[i + 1])
        o_ref[0, qi, h * HEAD_DIM:(h + 1) * HEAD_DIM, :] = attend(qi, h, s)
        if not bias_all:
            step = next(bias_steps)
            if step is not None:
                bias_all.append(step)
        yield qi + 1


def _sb_slabs(q_ref, kt_ref, v_ref, o_ref, acc_ref, carry_ref):
    t = KEY_BLOCK
    half = t // 2
    rows = HEADS_PER_TILE * half
    n_tiles = q_ref.shape[1] // t
    masks = _head_masks((half, LANES))
    tri = jnp.where(lax.broadcasted_iota(jnp.int32, (t, t), 0) > lax.broadcasted_iota(jnp.int32, (t, t), 1),
                    1.0, 0.0).astype(BF16)

    def earlier_keys(nk, first_query):
        q_pos = (lax.broadcasted_iota(jnp.int32, (rows, nk), 0) & (half - 1)) + first_query
        return lax.broadcasted_iota(jnp.int32, (rows, nk), 1) < q_pos

    def logits(q_start, block, nk):
        q2 = q_ref[0, pl.ds(q_start, half), :]
        qs = jnp.concatenate([jnp.where(m, q2, jnp.zeros_like(q2)) for m in masks], axis=0)
        return _dot(qs, kt_ref[0, block, :, 0:nk])

    def decay(z, nk, earlier):
        if earlier is not None:
            z = jnp.where(earlier, z, NEG)
        log_beta = jnp.minimum(z, 0.0) - jnp.log2(1.0 + jnp.exp2(-jnp.abs(z)))
        log_1mb = log_beta - z
        suffix = _dot(log_1mb.astype(BF16), tri[:nk, :nk])
        total = suffix[:, 0:1] + log_1mb[:, 0:1]
        return log_beta, suffix, jnp.broadcast_to(total, (rows, LANES))

    def weighted_values(log_beta, suffix, block, nk):
        v2 = v_ref[0, pl.ds(pl.multiple_of(block * t, t), nk), :]
        return _dot(jnp.exp2(log_beta + suffix).astype(BF16), v2)

    def weigh(z, block, nk, earlier):
        log_beta, suffix, total = decay(z, nk, earlier)
        return weighted_values(log_beta, suffix, block, nk), total

    def alive(slot):
        return (jnp.max(carry_ref[slot]) > SB_DEAD_LOG2).astype(jnp.int32)

    def finish(slot, q_start, first_block, live):
        def more(state):
            n, live = state
            return (n >= 0) & (live > 0)

        def earlier_block(state):
            n, _ = state
            pv, total = weigh(logits(q_start, n, t), n, t, None)
            carry = carry_ref[slot]
            acc_ref[slot] = acc_ref[slot] + jnp.exp2(carry) * pv
            carry_ref[slot] = carry + total
            return n - 1, alive(slot)

        if first_block >= 0:
            lax.while_loop(more, earlier_block, (jnp.int32(first_block), live))
        acc = acc_ref[slot]
        o_ref[0, q_start:q_start + half, :] = _merge_heads(
            [acc[h * half:(h + 1) * half] for h in range(HEADS_PER_TILE)], masks).astype(o_ref.dtype)

    def tile_group(tiles):
        slabs = []
        pending = []
        for i, qi in enumerate(tiles):
            top = qi * t
            slabs.append((2 * i, top, qi, half, earlier_keys(half, 0)))
            if qi:
                slabs.append((2 * i, top, qi - 1, t, None))
            slabs.append((2 * i + 1, top + half, qi, t, earlier_keys(t, half)))
            pending += [(2 * i, top, qi - 2), (2 * i + 1, top + half, qi - 1)]
        n = len(slabs)
        z = {j: logits(*slabs[j][1:4]) for j in range(min(2, n))}
        mid = {0: decay(z.pop(0), slabs[0][3], slabs[0][4])}
        started = set()
        for j, (slot, q_start, block, nk, mask) in enumerate(slabs):
            if j + 2 < n:
                z[j + 2] = logits(*slabs[j + 2][1:4])
            if j + 1 < n:
                mid[j + 1] = decay(z.pop(j + 1), slabs[j + 1][3], slabs[j + 1][4])
            log_beta, suffix, total = mid.pop(j)
            pv = weighted_values(log_beta, suffix, block, nk)
            if slot in started:
                carry = carry_ref[slot]
                acc_ref[slot] = acc_ref[slot] + jnp.exp2(carry) * pv
                carry_ref[slot] = carry + total
            else:
                acc_ref[slot] = pv
                carry_ref[slot] = total
                started.add(slot)
            yield 1
        lives = [alive(slot) if first_block >= 0 else None for slot, _, first_block in pending]
        for (slot, q_start, first_block), live in zip(pending, lives):
            finish(slot, q_start, first_block, live)

    for first in range(0, n_tiles, SB_TILE_GROUP):
        yield from tile_group(range(first, first + SB_TILE_GROUP))


def _mixers_kernel(qa_ref, kat_ref, vat_ref, km_ref, qb_ref, kbt_ref, vb_ref, ym_ref, ys_ref,
                   kaug_ref, vaug_ref, acc_ref, carry_ref):
    streams = [
        [_moba_chains(qa_ref, kat_ref, vat_ref, km_ref, ym_ref, kaug_ref, vaug_ref), 0.0, MOBA_WORK],
        [_sb_slabs(qb_ref, kbt_ref, vb_ref, ys_ref, acc_ref, carry_ref), 0.0, SB_WORK],
    ]
    while streams:
        stream = min(streams, key=lambda s: s[1] / s[2])
        try:
            stream[1] += next(stream[0])
        except StopIteration:
            streams.remove(stream)


def _token_mixers(qa, kat, vat, kmean, qb, kbt, vb):
    b, s, w = qa.shape
    nkb = s // KEY_BLOCK
    tok = pl.BlockSpec((1, s, LANES), lambda bi, hp: (bi, 0, hp))
    feat = pl.BlockSpec((1, nkb, LANES, KEY_BLOCK), lambda bi, hp: (bi, 0, hp, 0))
    km_spec = pl.BlockSpec((1, nkb, LANES), lambda bi, hp: (bi, 0, hp))
    return pl.pallas_call(
        _mixers_kernel,
        grid=(b, w // LANES),
        in_specs=[tok, feat, feat, km_spec, tok, feat, tok],
        out_specs=[feat, tok],
        out_shape=[jax.ShapeDtypeStruct(vat.shape, BF16), jax.ShapeDtypeStruct(vb.shape, BF16)],
        scratch_shapes=[
            pltpu.VMEM((HEADS_PER_TILE, LANES, s), BF16),
            pltpu.VMEM((HEADS_PER_TILE, HEAD_DIM + SUBLANES, s), BF16),
            pltpu.VMEM((2 * SB_TILE_GROUP, KEY_BLOCK, LANES), F32),
            pltpu.VMEM((2 * SB_TILE_GROUP, KEY_BLOCK, LANES), F32),
        ],
        compiler_params=pltpu.CompilerParams(
            dimension_semantics=("arbitrary", "arbitrary"), vmem_limit_bytes=VMEM_LIMIT),
    )(qa, kat, vat, kmean, qb, kbt, vb)


def _out_kernel(x_ref, ym_ref, ys_ref, gate_ref, wum_ref, wus_ref, wo_ref, gpm_ref,
                gpre_ref, wmi_ref, wmo_ref, gpost_ref, o_ref):
    d = x_ref.shape[-1]
    ff = wmi_ref.shape[1]
    rows = x_ref.shape[0] // OUT_CHAINS
    assert rows == KEY_BLOCK
    chains = [pl.ds(c * rows, rows) for c in range(OUT_CHAINS)]

    def mix(c, r):
        up_m = lax.dot_general(ym_ref[c], wum_ref[...], (((0,), (0,)), ((), ())), preferred_element_type=F32)
        up_s = _dot(ys_ref[r, :], wus_ref[...])
        mixed = gate_ref[r, :d].astype(F32) * up_m + gate_ref[r, d:].astype(F32) * up_s
        return _dot(mixed.astype(BF16), wo_ref[...])

    def mlp_chunk(hn, j):
        u = _dot(hn, wmi_ref[:, j * d:(j + 1) * d])
        u = jnp.square(jnp.maximum(u, 0.0))
        return _dot(u.astype(BF16), wmo_ref[j * d:(j + 1) * d, :])

    mix_out = [mix(c, r) for c, r in enumerate(chains)]
    x1, hn, acc = [], [], []
    for c, r in enumerate(chains):
        x1.append(x_ref[r, :] + _rms(mix_out[c], gpm_ref[...]))
        hn.append(_rms(x1[c], gpre_ref[...]).astype(BF16))
        acc.append(mlp_chunk(hn[c], 0))
    for j in range(1, ff // d):
        for c in range(OUT_CHAINS):
            acc[c] = acc[c] + mlp_chunk(hn[c], j)
    for c, r in enumerate(chains):
        o_ref[r, :] = x1[c] + _rms(acc[c], gpost_ref[...])


def _output_block(x2, ym, ys, gate, wum, wus, wo, gpm, gpre, wmi, wmo, gpost):
    n, d = x2.shape
    tm = OUT_TILE
    tok = lambda i: (i, 0)
    const = lambda i: (0, 0)
    resident = lambda a: pl.BlockSpec(a.shape, const, pipeline_mode=pl.Buffered(1))
    return pl.pallas_call(
        _out_kernel,
        grid=(n // tm,),
        in_specs=[
            pl.BlockSpec((tm, d), tok),
            pl.BlockSpec((tm // KEY_BLOCK,) + ym.shape[1:], lambda i: (i, 0, 0)),
            pl.BlockSpec((tm, ys.shape[1]), tok),
            pl.BlockSpec((tm, gate.shape[1]), tok),
            resident(wum), resident(wus), resident(wo), resident(gpm),
            resident(gpre), resident(wmi), resident(wmo), resident(gpost),
        ],
        out_specs=pl.BlockSpec((tm, d), tok),
        out_shape=jax.ShapeDtypeStruct((n, d), F32),
        compiler_params=pltpu.CompilerParams(
            dimension_semantics=("arbitrary",), vmem_limit_bytes=VMEM_LIMIT),
    )(x2, ym, ys, gate, wum, wus, wo, gpm, gpre, wmi, wmo, gpost)


def _rope_tables(seq):
    half = HEAD_DIM // 2
    inv_freq = ROPE_THETA ** (-jnp.arange(half, dtype=F32) * 2.0 / HEAD_DIM)
    ang = jnp.arange(seq, dtype=F32)[:, None] * inv_freq[None, :]
    ang = jnp.concatenate([ang] * (2 * HEADS_PER_TILE), axis=-1)
    cos, sin = jnp.cos(ang), jnp.sin(ang)
    first_half = (jnp.arange(LANES) % HEAD_DIM) < half
    return cos, jnp.where(first_half, -sin, 0.0), jnp.where(first_half, 0.0, sin)


def kernel(x, g_pre_mix, w_in, b_gate, w_up_moba, w_up_sb, w_out, g_post_mix,
           g_pre_mlp, w_mlp_in, w_mlp_out, g_post_mlp):
    b, s, d = x.shape
    depth = w_in.shape[0]
    cos, sina, sinb = _rope_tables(s)
    for l in range(depth):
        w = BRANCH_WIDTH
        qa, kat, vat, qb, kbt, vb, gate, kmean = _in_projection(
            x, g_pre_mix[l][None], w_in[l].astype(BF16), b_gate[l][None], cos, sina, sinb)
        nkb = kmean.shape[1]
        y_moba, y_sb = _token_mixers(qa, kat, vat, kmean.reshape(b, nkb, -1), qb, kbt, vb)
        x = _output_block(
            x.reshape(b * s, d), y_moba.reshape(b * nkb, w, -1), y_sb.reshape(b * s, -1),
            gate.reshape(b * s, -1),
            w_up_moba[l].astype(BF16), w_up_sb[l].astype(BF16), w_out[l].astype(BF16),
            g_post_mix[l][None], g_pre_mlp[l][None],
            w_mlp_in[l].astype(BF16), w_mlp_out[l].astype(BF16), g_post_mlp[l][None],
        ).reshape(b, s, d)
    return x
```
